```python
import math
import jax, jax.numpy as jnp
from jax import lax
import numpy as np

D_MODEL = 2048
BATCH = 2
SEQ = 16384
DEPTH = 4

N_MEM = 256
N_MIXERS = 2
N_DIFF_LAYERS = (DEPTH + 1) // 2
N_MLA_LAYERS = DEPTH // 2
ROPE_THETA = 500000.0
Q_BLOCK = 128
EPS = 1e-6
DA_HEAD_DIM = 128
DA_HEADS = D_MODEL // (2 * DA_HEAD_DIM)
DA_ROT = DA_HEAD_DIM // 4
MLA_NOPE = 128
MLA_ROPE = 64
MLA_V = 128
MLA_HEADS = D_MODEL // MLA_V
MLA_Q_LORA = D_MODEL // 4
MLA_KV_LORA = D_MODEL // 4
CA_HEADS = 4
CA_HEAD_DIM = 128
D_FF = 4 * D_MODEL

kernel_name = "hybrid_diffattn_mla_memxattn_sqrelu"


def rms_norm(x, g):
    xf = x.astype(jnp.float32)
    y = xf * lax.rsqrt(jnp.mean(xf * xf, axis=-1, keepdims=True) + EPS)
    return (y * g.astype(jnp.float32)).astype(x.dtype)


def rope_angles(positions, rot_dim):
    inv_freq = ROPE_THETA ** (-jnp.arange(0, rot_dim, 2, dtype=jnp.float32) / rot_dim)
    ang = positions.astype(jnp.float32)[..., None] * inv_freq
    return jnp.cos(ang)[:, :, None, :], jnp.sin(ang)[:, :, None, :]


def apply_rope(x, cos, sin):
    xf = x.astype(jnp.float32)
    x1, x2 = jnp.split(xf, 2, axis=-1)
    return jnp.concatenate([x1 * cos - x2 * sin, x2 * cos + x1 * sin], axis=-1).astype(x.dtype)


def partial_rope(x, cos, sin, rot):
    return jnp.concatenate([apply_rope(x[..., :rot], cos, sin), x[..., rot:]], axis=-1)


def causal_mask(q_start, s_len):
    qpos = q_start + jnp.arange(Q_BLOCK)
    kpos = jnp.arange(s_len)
    return kpos[None, :] <= qpos[:, None]


def causal_block_sweep(block_fn, q):
    B, S = q.shape[:2]
    nb = S // Q_BLOCK
    qb = jnp.moveaxis(q.reshape((B, nb, Q_BLOCK) + q.shape[2:]), 1, 0)
    out = lax.map(lambda a: block_fn(a[0], a[1]), (qb, jnp.arange(nb) * Q_BLOCK))
    return jnp.moveaxis(out, 0, 1).reshape((B, S) + out.shape[3:])


def diff_attention(h, wqkv, lam_vecs, subln, wo, cos, sin, lambda_init):
    B, S, _ = h.shape
    q, k, v = jnp.split(h @ wqkv, 3, axis=-1)
    q = partial_rope(q.reshape(B, S, 2 * DA_HEADS, DA_HEAD_DIM), cos, sin, DA_ROT)
    k = partial_rope(k.reshape(B, S, 2 * DA_HEADS, DA_HEAD_DIM), cos, sin, DA_ROT)
    q = q.reshape(B, S, DA_HEADS, 2, DA_HEAD_DIM)
    k = k.reshape(B, S, DA_HEADS, 2, DA_HEAD_DIM)
    v = v.reshape(B, S, DA_HEADS, 2 * DA_HEAD_DIM)
    lv = lam_vecs.astype(jnp.float32)
    lam = jnp.exp(jnp.sum(lv[0] * lv[1])) - jnp.exp(jnp.sum(lv[2] * lv[3])) + lambda_init
    scale = 1.0 / math.sqrt(DA_HEAD_DIM)

    def block(qi, q0):
        s = jnp.einsum('bqhcd,bkhcd->bhcqk', qi, k).astype(jnp.float32) * scale
        s = jnp.where(causal_mask(q0, S), s, -jnp.inf)
        p = jax.nn.softmax(s, axis=-1)
        a = (p[:, :, 0] - lam * p[:, :, 1]).astype(v.dtype)
        return jnp.einsum('bhqk,bkhd->bqhd', a, v)

    o = causal_block_sweep(block, q)
    o = rms_norm(o, subln) * (1.0 - lambda_init)
    return o.reshape(B, S, D_MODEL) @ wo


def mla_attention(h, wdown, q_norm, kv_norm, wuq, wukv, wo, cos, sin):
    B, S, _ = h.shape
    d = h @ wdown
    c_q = rms_norm(d[..., :MLA_Q_LORA], q_norm)
    c_kv = rms_norm(d[..., MLA_Q_LORA:MLA_Q_LORA + MLA_KV_LORA], kv_norm)
    k_rope = apply_rope(d[..., MLA_Q_LORA + MLA_KV_LORA:][:, :, None, :], cos, sin)
    q = (c_q @ wuq).reshape(B, S, MLA_HEADS, MLA_NOPE + MLA_ROPE)
    q = jnp.concatenate([q[..., :MLA_NOPE], apply_rope(q[..., MLA_NOPE:], cos, sin)], axis=-1)
    kv = (c_kv @ wukv).reshape(B, S, MLA_HEADS, MLA_NOPE + MLA_V)
    k = jnp.concatenate([kv[..., :MLA_NOPE],
                         jnp.broadcast_to(k_rope, (B, S, MLA_HEADS, MLA_ROPE))], axis=-1)
    v = kv[..., MLA_NOPE:]
    scale = 1.0 / math.sqrt(MLA_NOPE + MLA_ROPE)

    def block(qi, q0):
        s = jnp.einsum('bqhd,bkhd->bhqk', qi, k).astype(jnp.float32) * scale
        s = jnp.where(causal_mask(q0, S), s, -jnp.inf)
        p = jax.nn.softmax(s, axis=-1).astype(v.dtype)
        return jnp.einsum('bhqk,bkhd->bqhd', p, v)

    o = causal_block_sweep(block, q)
    return o.reshape(B, S, MLA_HEADS * MLA_V) @ wo


def memory_cross_attention(h, mem_h, wq, wkv, wo):
    B, S, _ = h.shape
    M = mem_h.shape[1]
    q = (h @ wq).reshape(B, S, CA_HEADS, CA_HEAD_DIM)
    k, v = jnp.split(mem_h @ wkv, 2, axis=-1)
    k = k.reshape(B, M, CA_HEADS, CA_HEAD_DIM)
    v = v.reshape(B, M, CA_HEADS, CA_HEAD_DIM)
    s = jnp.einsum('bqhd,bmhd->bhqm', q, k).astype(jnp.float32) * (1.0 / math.sqrt(CA_HEAD_DIM))
    p = jax.nn.softmax(s, axis=-1).astype(v.dtype)
    o = jnp.einsum('bhqm,bmhd->bqhd', p, v)
    return o.reshape(B, S, CA_HEADS * CA_HEAD_DIM) @ wo


def sq_relu_mlp(h, wup, wdown):
    return jnp.square(jax.nn.relu(h @ wup)) @ wdown


def _normal(k, shape, scale):
    return jax.random.normal(k, shape, jnp.float32) * scale


def _gain(k, shape):
    return 1.0 + 0.02 * jax.random.normal(k, shape, jnp.float32)


def setup_inputs(seed: int = 0) -> dict:
    key = jax.random.key(seed)
    ks = jax.random.split(key, 22)
    D = D_MODEL
    return {
        "x": _normal(ks[0], (BATCH, SEQ, D), 1.0),
        "mem": _normal(ks[1], (BATCH, N_MEM, D), 1.0),
        "positions": jnp.broadcast_to(jnp.arange(SEQ, dtype=jnp.int32), (BATCH, SEQ)),
        "attn_norm": _gain(ks[2], (DEPTH, D)),
        "cross_norm": _gain(ks[3], (DEPTH, D)),
        "mlp_norm": _gain(ks[4], (DEPTH, D)),
        "mem_norm": _gain(ks[5], (D,)),
        "final_norm": _gain(ks[6], (D,)),
        "da_wqkv": _normal(ks[7], (N_DIFF_LAYERS, D, 3 * D), D ** -0.5),
        "da_lambda": _normal(ks[8], (N_DIFF_LAYERS, 4, DA_HEAD_DIM), 0.1),
        "da_subln": _gain(ks[9], (N_DIFF_LAYERS, 2 * DA_HEAD_DIM)),
        "da_wo": _normal(ks[10], (N_DIFF_LAYERS, D, D), D ** -0.5),
        "mla_wdown": _normal(ks[11], (N_MLA_LAYERS, D, MLA_Q_LORA + MLA_KV_LORA + MLA_ROPE), D ** -0.5),
        "mla_q_norm": _gain(ks[12], (N_MLA_LAYERS, MLA_Q_LORA)),
        "mla_kv_norm": _gain(ks[13], (N_MLA_LAYERS, MLA_KV_LORA)),
        "mla_wuq": _normal(ks[14], (N_MLA_LAYERS, MLA_Q_LORA, MLA_HEADS * (MLA_NOPE + MLA_ROPE)), MLA_Q_LORA ** -0.5),
        "mla_wukv": _normal(ks[15], (N_MLA_LAYERS, MLA_KV_LORA, MLA_HEADS * (MLA_NOPE + MLA_V)), MLA_KV_LORA ** -0.5),
        "mla_wo": _normal(ks[16], (N_MLA_LAYERS, MLA_HEADS * MLA_V, D), (MLA_HEADS * MLA_V) ** -0.5),
        "ca_wq": _normal(ks[17], (DEPTH, D, CA_HEADS * CA_HEAD_DIM), D ** -0.5),
        "ca_wkv": _normal(ks[18], (DEPTH, D, 2 * CA_HEADS * CA_HEAD_DIM), D ** -0.5),
        "ca_wo": _normal(ks[19], (DEPTH, CA_HEADS * CA_HEAD_DIM, D), (CA_HEADS * CA_HEAD_DIM) ** -0.5),
        "mlp_wup": _normal(ks[20], (DEPTH, D, D_FF), D ** -0.5),
        "mlp_wdown": _normal(ks[21], (DEPTH, D_FF, D), D_FF ** -0.5),
    }


def reference(x, mem, positions, attn_norm, cross_norm, mlp_norm, mem_norm, final_norm,
              da_wqkv, da_lambda, da_subln, da_wo,
              mla_wdown, mla_q_norm, mla_kv_norm, mla_wuq, mla_wukv, mla_wo,
              ca_wq, ca_wkv, ca_wo, mlp_wup, mlp_wdown):
    cos_p, sin_p = rope_angles(positions, DA_ROT)
    cos_m, sin_m = rope_angles(positions, MLA_ROPE)
    mem_h = rms_norm(mem, mem_norm)
    for i in range(DEPTH):
        j = i // N_MIXERS
        h = rms_norm(x, attn_norm[i])
        if i % N_MIXERS == 0:
            lambda_init = 0.8 - 0.6 * math.exp(-0.3 * i)
            x = x + diff_attention(h, da_wqkv[j], da_lambda[j], da_subln[j], da_wo[j],
                                   cos_p, sin_p, lambda_init)
        else:
            x = x + mla_attention(h, mla_wdown[j], mla_q_norm[j], mla_kv_norm[j],
                                  mla_wuq[j], mla_wukv[j], mla_wo[j], cos_m, sin_m)
        x = x + memory_cross_attention(rms_norm(x, cross_norm[i]), mem_h,
                                       ca_wq[i], ca_wkv[i], ca_wo[i])
        x = x + sq_relu_mlp(rms_norm(x, mlp_norm[i]), mlp_wup[i], mlp_wdown[i])
    return rms_norm(x, final_norm)
```

```python
import functools
import math

import jax
import jax.numpy as jnp
from jax import lax
from jax.experimental import pallas as pl
from jax.experimental.pallas import tpu as pltpu

F32 = jnp.float32
BF16 = jnp.bfloat16

EPS = 1e-6
ROPE_THETA = 500000.0
LANES = 128
LOG2E = 1.4426950408889634
NEG_BIG = -1e30

DA_HEAD_DIM = 128
DA_ROT = DA_HEAD_DIM // 4
MLA_NOPE = 128
MLA_ROPE = 64
MLA_V = 128
MLA_QK_PAD = 256
CA_HEADS = 4
CA_HEAD_DIM = 128

VMEM_LIMIT = 56 * 1024 * 1024


def _cparams(*sem):
    return pltpu.CompilerParams(dimension_semantics=sem, vmem_limit_bytes=VMEM_LIMIT)


def _pick(n, target):
    t = min(n, target)
    while n % t or (t % 8 and t != n):
        t -= 1
    return t


def _rms(xf, g):
    ms = jnp.mean(xf * xf, axis=-1, keepdims=True)
    return xf * lax.rsqrt(ms + EPS) * g


def _rope_chunk(xc, tab, shift):
    c = tab[:, 0:LANES]
    s1 = tab[:, LANES:2 * LANES]
    s2 = tab[:, 2 * LANES:3 * LANES]
    return xc * c + pltpu.roll(xc, LANES - shift, 1) * s1 + pltpu.roll(xc, shift, 1) * s2


def _rope_table(positions, rot):
    half = rot // 2
    inv_freq = ROPE_THETA ** (-jnp.arange(0, rot, 2, dtype=F32) / rot)
    ang = positions.astype(F32).reshape(-1)[:, None] * inv_freq
    cos, sin = jnp.cos(ang), jnp.sin(ang)
    n = ang.shape[0]
    c = jnp.concatenate([cos, cos, jnp.ones((n, LANES - rot), F32)], axis=1)
    s1 = jnp.concatenate([-sin, jnp.zeros((n, LANES - half), F32)], axis=1)
    s2 = jnp.concatenate([jnp.zeros((n, half), F32), sin, jnp.zeros((n, LANES - rot), F32)], axis=1)
    return jnp.concatenate([c, s1, s2], axis=1)


def _norm_matmul_kernel(x_ref, g_ref, w_ref, o_ref, h_ref):
    @pl.when(pl.program_id(1) == 0)
    def _():
        h_ref[...] = _rms(x_ref[...], g_ref[...]).astype(BF16)

    o_ref[...] = jnp.dot(h_ref[...], w_ref[...], preferred_element_type=F32).astype(o_ref.dtype)


def _norm_matmul(x, g, w, out_dtype, tm_target=512, tn_target=1024):
    m, d = x.shape
    n = w.shape[1]
    tm, tn = _pick(m, tm_target), _pick(n, tn_target)
    return pl.pallas_call(
        _norm_matmul_kernel,
        grid=(m // tm, n // tn),
        in_specs=[pl.BlockSpec((tm, d), lambda i, j: (i, 0)),
                  pl.BlockSpec((1, d), lambda i, j: (0, 0)),
                  pl.BlockSpec((d, tn), lambda i, j: (0, j))],
        out_specs=pl.BlockSpec((tm, tn), lambda i, j: (i, j)),
        out_shape=jax.ShapeDtypeStruct((m, n), out_dtype),
        scratch_shapes=[pltpu.VMEM((tm, d), BF16)],
        compiler_params=_cparams("parallel", "arbitrary"),
        name="norm_matmul",
    )(x, g.reshape(1, d), w)


def _da_qkv_kernel(x_ref, g_ref, w_ref, tab_ref, o_ref, h_ref, *, n_head_blocks, q_scale):
    j = pl.program_id(1)

    @pl.when(j == 0)
    def _():
        h_ref[...] = _rms(x_ref[...], g_ref[...]).astype(BF16)

    acc = jnp.dot(h_ref[...], w_ref[...], preferred_element_type=F32)
    n_chunks = acc.shape[1] // LANES

    def store_rope(scale):
        tab = tab_ref[...]
        for c in range(n_chunks):
            y = _rope_chunk(acc[:, c * LANES:(c + 1) * LANES], tab, DA_ROT // 2)
            if scale is not None:
                y = y * scale
            o_ref[:, c * LANES:(c + 1) * LANES] = y.astype(o_ref.dtype)

    @pl.when(j < n_head_blocks)
    def _():
        store_rope(q_scale)

    @pl.when(jnp.logical_and(j >= n_head_blocks, j < 2 * n_head_blocks))
    def _():
        store_rope(None)

    @pl.when(j >= 2 * n_head_blocks)
    def _():
        o_ref[...] = acc.astype(o_ref.dtype)


def _da_qkv(x, g, w, tab, tm_target=512, tn_target=1024):
    m, d = x.shape
    n = w.shape[1]
    tm, tn = _pick(m, tm_target), _pick(d, tn_target)
    q_scale = LOG2E / math.sqrt(DA_HEAD_DIM)
    kern = functools.partial(_da_qkv_kernel, n_head_blocks=d // tn, q_scale=q_scale)
    return pl.pallas_call(
        kern,
        grid=(m // tm, n // tn),
        in_specs=[pl.BlockSpec((tm, d), lambda i, j: (i, 0)),
                  pl.BlockSpec((1, d), lambda i, j: (0, 0)),
                  pl.BlockSpec((d, tn), lambda i, j: (0, j)),
                  pl.BlockSpec((tm, 3 * LANES), lambda i, j: (i, 0))],
        out_specs=pl.BlockSpec((tm, tn), lambda i, j: (i, j)),
        out_shape=jax.ShapeDtypeStruct((m, n), BF16),
        scratch_shapes=[pltpu.VMEM((tm, d), BF16)],
        compiler_params=_cparams("parallel", "arbitrary"),
        name="da_qkv",
    )(x, g.reshape(1, d), w, tab)


def _da_attn_kernel(q_ref, k_ref, v_ref, lam_ref, sub_ref, o_ref, m_ref, l_ref, acc_ref,
                    *, tq, tk, lambda_init):
    qi = pl.program_id(2)
    hd = DA_HEAD_DIM
    m_ref[...] = jnp.full(m_ref.shape, NEG_BIG, F32)
    l_ref[...] = jnp.zeros(l_ref.shape, F32)
    acc_ref[...] = jnp.zeros(acc_ref.shape, F32)
    q = q_ref[...]

    def step(ki, masked):
        off = pl.multiple_of(ki * tk, tk)
        k = k_ref[pl.ds(off, tk), :]
        v = v_ref[pl.ds(off, tk), :]
        for c in range(2):
            s = lax.dot_general(q[:, c * hd:(c + 1) * hd], k[:, c * hd:(c + 1) * hd],
                                (((1,), (1,)), ((), ())), preferred_element_type=F32)
            if masked:
                row = lax.broadcasted_iota(jnp.int32, (tq, tk), 0)
                col = lax.broadcasted_iota(jnp.int32, (tq, tk), 1)
                s = jnp.where(col <= row, s, NEG_BIG)
            m_prev = m_ref[c]
            m_next = jnp.maximum(m_prev, jnp.max(s, axis=1, keepdims=True))
            alpha = jnp.exp2(m_prev - m_next)
            p = jnp.exp2(s - jnp.tile(m_next, (1, tk // LANES)))
            l_ref[c] = alpha * l_ref[c] + jnp.sum(p, axis=1, keepdims=True)
            m_ref[c] = m_next
            pv = jnp.dot(p.astype(BF16), v, preferred_element_type=F32)
            acc_ref[c] = acc_ref[c] * jnp.tile(alpha, (1, 2 * hd // LANES)) + pv

    def body(ki, carry):
        step(ki, False)
        return carry

    lax.fori_loop(0, qi, body, 0)
    step(qi, True)

    lv = lam_ref[...]
    lam = (jnp.exp(jnp.sum(lv[0:1] * lv[1:2], axis=1, keepdims=True))
           - jnp.exp(jnp.sum(lv[2:3] * lv[3:4], axis=1, keepdims=True)) + lambda_init)
    rep = 2 * hd // LANES
    o = (acc_ref[0] / jnp.tile(l_ref[0], (1, rep))
         - lam * (acc_ref[1] / jnp.tile(l_ref[1], (1, rep))))
    o_ref[...] = (_rms(o, sub_ref[...]) * (1.0 - lambda_init)).astype(o_ref.dtype)


def _da_attention(qkv, lam_vecs, subln, lambda_init, batch, seq, blk_target=512):
    d3 = qkv.shape[1]
    d = d3 // 3
    hw = 2 * DA_HEAD_DIM
    heads = d // hw
    blk = _pick(seq, blk_target)
    qkv3 = qkv.reshape(batch, seq, d3)
    kern = functools.partial(_da_attn_kernel, tq=blk, tk=blk, lambda_init=lambda_init)
    out = pl.pallas_call(
        kern,
        grid=(batch, heads, seq // blk),
        in_specs=[pl.BlockSpec((None, blk, hw), lambda b, h, i: (b, i, h)),
                  pl.BlockSpec((None, seq, hw), lambda b, h, i: (b, 0, heads + h)),
                  pl.BlockSpec((None, seq, hw), lambda b, h, i: (b, 0, 2 * heads + h)),
                  pl.BlockSpec((4, DA_HEAD_DIM), lambda b, h, i: (0, 0)),
                  pl.BlockSpec((1, hw), lambda b, h, i: (0, 0))],
        out_specs=pl.BlockSpec((None, blk, hw), lambda b, h, i: (b, i, h)),
        out_shape=jax.ShapeDtypeStruct((batch, seq, d), BF16),
        scratch_shapes=[pltpu.VMEM((2, blk, LANES), F32),
                        pltpu.VMEM((2, blk, LANES), F32),
                        pltpu.VMEM((2, blk, hw), F32)],
        compiler_params=_cparams("parallel", "parallel", "arbitrary"),
        name="da_attn",
    )(qkv3, qkv3, qkv3, lam_vecs, subln.reshape(1, hw))
    return out.reshape(batch * seq, d)


def _mla_proj_kernel(x_ref, g_ref, wd_ref, qn_ref, kvn_ref, wuq_ref, wukv_ref, tab_ref,
                     q_ref, kv_ref, kr_ref, *, q_lora, kv_lora, heads, q_scale):
    h = _rms(x_ref[...], g_ref[...]).astype(BF16)
    d = jnp.dot(h, wd_ref[...], preferred_element_type=F32)
    tab = tab_ref[...]
    c_q = _rms(d[:, :q_lora], qn_ref[...]).astype(BF16)
    c_kv = _rms(d[:, q_lora:q_lora + kv_lora], kvn_ref[...]).astype(BF16)
    kr = _rope_chunk(d[:, q_lora + kv_lora:q_lora + kv_lora + LANES], tab, MLA_ROPE // 2)
    kr_ref[...] = kr.astype(kr_ref.dtype)
    w = MLA_QK_PAD
    for hh in range(heads):
        qh = jnp.dot(c_q, wuq_ref[:, hh * w:(hh + 1) * w], preferred_element_type=F32)
        q_ref[:, hh * w:hh * w + LANES] = (qh[:, :LANES] * q_scale).astype(q_ref.dtype)
        qr = _rope_chunk(qh[:, LANES:], tab, MLA_ROPE // 2) * q_scale
        q_ref[:, hh * w + LANES:(hh + 1) * w] = qr.astype(q_ref.dtype)
        kv_ref[:, hh * w:(hh + 1) * w] = jnp.dot(
            c_kv, wukv_ref[:, hh * w:(hh + 1) * w], preferred_element_type=F32).astype(kv_ref.dtype)


def _mla_proj(x, g, wd_pad, qn, kvn, wuq_pad, wukv, tab, heads, tm_target=256):
    m, d = x.shape
    tm = _pick(m, tm_target)
    q_lora, kv_lora = qn.shape[0], kvn.shape[0]
    q_scale = LOG2E / math.sqrt(MLA_NOPE + MLA_ROPE)
    kern = functools.partial(_mla_proj_kernel, q_lora=q_lora, kv_lora=kv_lora, heads=heads, q_scale=q_scale)
    const = lambda i: (0, 0)
    row = lambda i: (i, 0)
    nq, nkv = wuq_pad.shape[1], wukv.shape[1]
    return pl.pallas_call(
        kern,
        grid=(m // tm,),
        in_specs=[pl.BlockSpec((tm, d), row),
                  pl.BlockSpec((1, d), const),
                  pl.BlockSpec(wd_pad.shape, const),
                  pl.BlockSpec((1, q_lora), const),
                  pl.BlockSpec((1, kv_lora), const),
                  pl.BlockSpec(wuq_pad.shape, const),
                  pl.BlockSpec(wukv.shape, const),
                  pl.BlockSpec((tm, 3 * LANES), row)],
        out_specs=[pl.BlockSpec((tm, nq), row),
                   pl.BlockSpec((tm, nkv), row),
                   pl.BlockSpec((tm, LANES), row)],
        out_shape=[jax.ShapeDtypeStruct((m, nq), BF16),
                   jax.ShapeDtypeStruct((m, nkv), BF16),
                   jax.ShapeDtypeStruct((m, LANES), BF16)],
        compiler_params=_cparams("parallel"),
        name="mla_proj",
    )(x, g.reshape(1, d), wd_pad, qn.reshape(1, q_lora), kvn.reshape(1, kv_lora), wuq_pad, wukv, tab)


def _mla_attn_kernel(q_ref, kn_ref, kr_ref, v_ref, o_ref, m_ref, l_ref, acc_ref, *, tq, tk):
    qi = pl.program_id(2)
    m_ref[...] = jnp.full(m_ref.shape, NEG_BIG, F32)
    l_ref[...] = jnp.zeros(l_ref.shape, F32)
    acc_ref[...] = jnp.zeros(acc_ref.shape, F32)
    q = q_ref[...]

    def step(ki, masked):
        off = pl.multiple_of(ki * tk, tk)
        k = jnp.concatenate([kn_ref[pl.ds(off, tk), :], kr_ref[pl.ds(off, tk), :]], axis=1)
        v = v_ref[pl.ds(off, tk), :]
        s = lax.dot_general(q, k, (((1,), (1,)), ((), ())), preferred_element_type=F32)
        if masked:
            row = lax.broadcasted_iota(jnp.int32, (tq, tk), 0)
            col = lax.broadcasted_iota(jnp.int32, (tq, tk), 1)
            s = jnp.where(col <= row, s, NEG_BIG)
        m_prev = m_ref[...]
        m_next = jnp.maximum(m_prev, jnp.max(s, axis=1, keepdims=True))
        alpha = jnp.exp2(m_prev - m_next)
        p = jnp.exp2(s - jnp.tile(m_next, (1, tk // LANES)))
        l_ref[...] = alpha * l_ref[...] + jnp.sum(p, axis=1, keepdims=True)
        m_ref[...] = m_next
        acc_ref[...] = acc_ref[...] * alpha + jnp.dot(p.astype(BF16), v, preferred_element_type=F32)

    def body(ki, carry):
        step(ki, False)
        return carry

    lax.fori_loop(0, qi, body, 0)
    step(qi, True)
    o_ref[...] = (acc_ref[...] / l_ref[...]).astype(o_ref.dtype)


def _mla_attention(q, kv, kr, batch, seq, heads, blk_target=512):
    blk = _pick(seq, blk_target)
    w = MLA_QK_PAD
    q3 = q.reshape(batch, seq, heads * w)
    kv3 = kv.reshape(batch, seq, heads * w)
    kr3 = kr.reshape(batch, seq, LANES)
    kern = functools.partial(_mla_attn_kernel, tq=blk, tk=blk)
    out = pl.pallas_call(
        kern,
        grid=(batch, heads, seq // blk),
        in_specs=[pl.BlockSpec((None, blk, w), lambda b, h, i: (b, i, h)),
                  pl.BlockSpec((None, seq, LANES), lambda b, h, i: (b, 0, 2 * h)),
                  pl.BlockSpec((None, seq, LANES), lambda b, h, i: (b, 0, 0)),
                  pl.BlockSpec((None, seq, LANES), lambda b, h, i: (b, 0, 2 * h + 1))],
        out_specs=pl.BlockSpec((None, blk, MLA_V), lambda b, h, i: (b, i, h)),
        out_shape=jax.ShapeDtypeStruct((batch, seq, heads * MLA_V), BF16),
        scratch_shapes=[pltpu.VMEM((blk, LANES), F32),
                        pltpu.VMEM((blk, LANES), F32),
                        pltpu.VMEM((blk, MLA_V), F32)],
        compiler_params=_cparams("parallel", "parallel", "arbitrary"),
        name="mla_attn",
    )(q3, kv3, kr3, kv3)
    return out.reshape(batch * seq, heads * MLA_V)


def _proj_residual_kernel(o_ref, w_ref, x_ref, out_ref):
    out_ref[...] = x_ref[...] + jnp.dot(o_ref[...], w_ref[...], preferred_element_type=F32)


def _proj_residual(o, w, x, tm_target=512, tn_target=1024):
    m, kdim = o.shape
    n = w.shape[1]
    tm, tn = _pick(m, tm_target), _pick(n, tn_target)
    return pl.pallas_call(
        _proj_residual_kernel,
        grid=(m // tm, n // tn),
        in_specs=[pl.BlockSpec((tm, kdim), lambda i, j: (i, 0)),
                  pl.BlockSpec((kdim, tn), lambda i, j: (0, j)),
                  pl.BlockSpec((tm, tn), lambda i, j: (i, j))],
        out_specs=pl.BlockSpec((tm, tn), lambda i, j: (i, j)),
        out_shape=jax.ShapeDtypeStruct((m, n), F32),
        compiler_params=_cparams("parallel", "parallel"),
        name="proj_residual",
    )(o, w, x)


def _cross_attn_kernel(x_ref, g_ref, wq_ref, kv_ref, wo_ref, out_ref, *, q_scale):
    x = x_ref[...]
    h = _rms(x, g_ref[...]).astype(BF16)
    q = (jnp.dot(h, wq_ref[...], preferred_element_type=F32) * q_scale).astype(BF16)
    hd = CA_HEAD_DIM
    kw = CA_HEADS * hd
    outs = []
    for hh in range(CA_HEADS):
        k = kv_ref[:, hh * hd:(hh + 1) * hd]
        v = kv_ref[:, kw + hh * hd:kw + (hh + 1) * hd]
        s = lax.dot_general(q[:, hh * hd:(hh + 1) * hd], k, (((1,), (1,)), ((), ())),
                            preferred_element_type=F32)
        p = jnp.exp2(s - jnp.max(s, axis=1, keepdims=True))
        l = jnp.sum(p, axis=1, keepdims=True)
        outs.append((jnp.dot(p.astype(BF16), v, preferred_element_type=F32) / l).astype(BF16))
    o = jnp.concatenate(outs, axis=1)
    out_ref[...] = x + jnp.dot(o, wo_ref[...], preferred_element_type=F32)


def _cross_attn(x, g, wq, kv_all, layer, wo, seq, tm_target=512):
    m, d = x.shape
    tm = _pick(seq, tm_target)
    blocks_per_batch = seq // tm
    n_mem = kv_all.shape[1]
    kvw = 2 * CA_HEADS * CA_HEAD_DIM
    kern = functools.partial(_cross_attn_kernel, q_scale=LOG2E / math.sqrt(CA_HEAD_DIM))
    return pl.pallas_call(
        kern,
        grid=(m // tm,),
        in_specs=[pl.BlockSpec((tm, d), lambda i: (i, 0)),
                  pl.BlockSpec((1, d), lambda i: (0, 0)),
                  pl.BlockSpec(wq.shape, lambda i: (0, 0)),
                  pl.BlockSpec((None, n_mem, kvw), lambda i: (i // blocks_per_batch, 0, layer)),
                  pl.BlockSpec(wo.shape, lambda i: (0, 0))],
        out_specs=pl.BlockSpec((tm, d), lambda i: (i, 0)),
        out_shape=jax.ShapeDtypeStruct((m, d), F32),
        compiler_params=_cparams("parallel"),
        name="cross_attn",
    )(x, g.reshape(1, d), wq, kv_all, wo)


def _mlp_kernel(x_ref, g_ref, wup_ref, wdown_ref, fg_ref, out_ref, h_ref, *, final_norm):
    f = pl.program_id(1)

    @pl.when(f == 0)
    def _():
        x = x_ref[...]
        h_ref[...] = _rms(x, g_ref[...]).astype(BF16)
        out_ref[...] = x

    u = jnp.maximum(jnp.dot(h_ref[...], wup_ref[...], preferred_element_type=F32), 0.0)
    out_ref[...] += jnp.dot((u * u).astype(BF16), wdown_ref[...], preferred_element_type=F32)

    if final_norm:
        @pl.when(f == pl.num_programs(1) - 1)
        def _():
            out_ref[...] = _rms(out_ref[...], fg_ref[...])


def _mlp(x, g, wup, wdown, final_g, final_norm, tm_target=1024, tf_target=512):
    m, d = x.shape
    ff = wup.shape[1]
    tm, tf = _pick(m, tm_target), _pick(ff, tf_target)
    kern = functools.partial(_mlp_kernel, final_norm=final_norm)
    return pl.pallas_call(
        kern,
        grid=(m // tm, ff // tf),
        in_specs=[pl.BlockSpec((tm, d), lambda i, f: (i, 0)),
                  pl.BlockSpec((1, d), lambda i, f: (0, 0)),
                  pl.BlockSpec((d, tf), lambda i, f: (0, f)),
                  pl.BlockSpec((tf, d), lambda i, f: (f, 0)),
                  pl.BlockSpec((1, d), lambda i, f: (0, 0))],
        out_specs=pl.BlockSpec((tm, d), lambda i, f: (i, 0)),
        out_shape=jax.ShapeDtypeStruct((m, d), F32),
        scratch_shapes=[pltpu.VMEM((tm, d), BF16)],
        compiler_params=_cparams("parallel", "arbitrary"),
        name="mlp",
    )(x, g.reshape(1, d), wup, wdown, final_g.reshape(1, d))


def kernel(x, mem, positions, attn_norm, cross_norm, mlp_norm, mem_norm, final_norm, da_wqkv, da_lambda, da_subln, da_wo, mla_wdown, mla_q_norm, mla_kv_norm, mla_wuq, mla_wukv, mla_wo, ca_wq, ca_wkv, ca_wo, mlp_wup, mlp_wdown):
    batch, seq, d = x.shape
    depth = attn_norm.shape[0]
    n_mem = mem.shape[1]
    mla_heads = mla_wo.shape[1] // MLA_V

    tab_da = _rope_table(positions, DA_ROT)
    tab_mla = _rope_table(positions, MLA_ROPE)

    wkv_all = jnp.concatenate([ca_wkv[i] for i in range(depth)], axis=1).astype(BF16)
    kv_all = _norm_matmul(mem.reshape(batch * n_mem, d), mem_norm, wkv_all, BF16)
    kv_all = kv_all.reshape(batch, n_mem, wkv_all.shape[1])

    xs = x.reshape(batch * seq, d)
    for i in range(depth):
        j = i // 2
        if i % 2 == 0:
            lambda_init = 0.8 - 0.6 * math.exp(-0.3 * i)
            qkv = _da_qkv(xs, attn_norm[i], da_wqkv[j].astype(BF16), tab_da)
            o = _da_attention(qkv, da_lambda[j], da_subln[j], lambda_init, batch, seq)
            xs = _proj_residual(o, da_wo[j].astype(BF16), xs)
        else:
            q_lora = mla_q_norm.shape[1]
            wd = mla_wdown[j]
            wd_pad = jnp.pad(wd, ((0, 0), (0, LANES - MLA_ROPE))).astype(BF16)
            wuq = mla_wuq[j].reshape(q_lora, mla_heads, MLA_NOPE + MLA_ROPE)
            wuq_pad = jnp.pad(wuq, ((0, 0), (0, 0), (0, MLA_QK_PAD - MLA_NOPE - MLA_ROPE)))
            wuq_pad = wuq_pad.reshape(q_lora, mla_heads * MLA_QK_PAD).astype(BF16)
            q, kv, kr = _mla_proj(xs, attn_norm[i], wd_pad, mla_q_norm[j], mla_kv_norm[j], wuq_pad,
                                  mla_wukv[j].astype(BF16), tab_mla, mla_heads)
            o = _mla_attention(q, kv, kr, batch, seq, mla_heads)
            xs = _proj_residual(o, mla_wo[j].astype(BF16), xs)
        xs = _cross_attn(xs, cross_norm[i], ca_wq[i].astype(BF16), kv_all, i, ca_wo[i].astype(BF16), seq)
        xs = _mlp(xs, mlp_norm[i], mlp_wup[i].astype(BF16), mlp_wdown[i].astype(BF16), final_norm,
                  final_norm=(i == depth - 1))
    return xs.reshape(batch, seq, d)
```

```python
import functools
import math

import jax
import jax.numpy as jnp
from jax import lax
from jax.experimental import pallas as pl
from jax.experimental.pallas import tpu as pltpu

F32 = jnp.float32
BF16 = jnp.bfloat16

EPS = 1e-6
ROPE_THETA = 500000.0
LANES = 128
LOG2E = 1.4426950408889634
NEG_BIG = -1e30

DA_HEAD_DIM = 128
DA_ROT = DA_HEAD_DIM // 4
MLA_NOPE = 128
MLA_ROPE = 64
MLA_V = 128
MLA_QK_PAD = 256
CA_HEADS = 4
CA_HEAD_DIM = 128

VMEM_LIMIT = 56 * 1024 * 1024


def _cparams(*sem):
    return pltpu.CompilerParams(dimension_semantics=sem, vmem_limit_bytes=VMEM_LIMIT)


def _pick(n, target):
    t = min(n, target)
    while n % t or (t % 8 and t != n):
        t -= 1
    return t


def _rms(xf, g):
    ms = jnp.mean(xf * xf, axis=-1, keepdims=True)
    return xf * lax.rsqrt(ms + EPS) * g


def _rope_chunk(xc, tab, shift):
    c = tab[:, 0:LANES]
    s1 = tab[:, LANES:2 * LANES]
    s2 = tab[:, 2 * LANES:3 * LANES]
    return xc * c + pltpu.roll(xc, LANES - shift, 1) * s1 + pltpu.roll(xc, shift, 1) * s2


def _rope_table(positions, rot):
    half = rot // 2
    inv_freq = ROPE_THETA ** (-jnp.arange(0, rot, 2, dtype=F32) / rot)
    ang = positions.astype(F32).reshape(-1)[:, None] * inv_freq
    cos, sin = jnp.cos(ang), jnp.sin(ang)
    n = ang.shape[0]
    c = jnp.concatenate([cos, cos, jnp.ones((n, LANES - rot), F32)], axis=1)
    s1 = jnp.concatenate([-sin, jnp.zeros((n, LANES - half), F32)], axis=1)
    s2 = jnp.concatenate([jnp.zeros((n, half), F32), sin, jnp.zeros((n, LANES - rot), F32)], axis=1)
    return jnp.concatenate([c, s1, s2], axis=1)


def _norm_matmul_kernel(x_ref, g_ref, w_ref, o_ref, h_ref):
    @pl.when(pl.program_id(1) == 0)
    def _():
        h_ref[...] = _rms(x_ref[...], g_ref[...]).astype(BF16)

    o_ref[...] = jnp.dot(h_ref[...], w_ref[...], preferred_element_type=F32).astype(o_ref.dtype)


def _norm_matmul(x, g, w, out_dtype, tm_target=512, tn_target=1024):
    m, d = x.shape
    n = w.shape[1]
    tm, tn = _pick(m, tm_target), _pick(n, tn_target)
    return pl.pallas_call(
        _norm_matmul_kernel,
        grid=(m // tm, n // tn),
        in_specs=[pl.BlockSpec((tm, d), lambda i, j: (i, 0)),
                  pl.BlockSpec((1, d), lambda i, j: (0, 0)),
                  pl.BlockSpec((d, tn), lambda i, j: (0, j))],
        out_specs=pl.BlockSpec((tm, tn), lambda i, j: (i, j)),
        out_shape=jax.ShapeDtypeStruct((m, n), out_dtype),
        scratch_shapes=[pltpu.VMEM((tm, d), BF16)],
        compiler_params=_cparams("parallel", "arbitrary"),
        name="norm_matmul",
    )(x, g.reshape(1, d), w)


def _da_qkv_kernel(x_ref, g_ref, w_ref, tab_ref, o_ref, h_ref, *, n_head_blocks, q_scale):
    j = pl.program_id(1)

    @pl.when(j == 0)
    def _():
        h_ref[...] = _rms(x_ref[...], g_ref[...]).astype(BF16)

    acc = jnp.dot(h_ref[...], w_ref[...], preferred_element_type=F32)
    n_chunks = acc.shape[1] // LANES

    def store_rope(scale):
        tab = tab_ref[...]
        for c in range(n_chunks):
            y = _rope_chunk(acc[:, c * LANES:(c + 1) * LANES], tab, DA_ROT // 2)
            if scale is not None:
                y = y * scale
            o_ref[:, c * LANES:(c + 1) * LANES] = y.astype(o_ref.dtype)

    @pl.when(j < n_head_blocks)
    def _():
        store_rope(q_scale)

    @pl.when(jnp.logical_and(j >= n_head_blocks, j < 2 * n_head_blocks))
    def _():
        store_rope(None)

    @pl.when(j >= 2 * n_head_blocks)
    def _():
        o_ref[...] = acc.astype(o_ref.dtype)


def _da_qkv(x, g, w, tab, tm_target=512, tn_target=1024):
    m, d = x.shape
    n = w.shape[1]
    tm, tn = _pick(m, tm_target), _pick(d, tn_target)
    q_scale = LOG2E / math.sqrt(DA_HEAD_DIM)
    kern = functools.partial(_da_qkv_kernel, n_head_blocks=d // tn, q_scale=q_scale)
    return pl.pallas_call(
        kern,
        grid=(m // tm, n // tn),
        in_specs=[pl.BlockSpec((tm, d), lambda i, j: (i, 0)),
                  pl.BlockSpec((1, d), lambda i, j: (0, 0)),
                  pl.BlockSpec((d, tn), lambda i, j: (0, j)),
                  pl.BlockSpec((tm, 3 * LANES), lambda i, j: (i, 0))],
        out_specs=pl.BlockSpec((tm, tn), lambda i, j: (i, j)),
        out_shape=jax.ShapeDtypeStruct((m, n), BF16),
        scratch_shapes=[pltpu.VMEM((tm, d), BF16)],
        compiler_params=_cparams("parallel", "arbitrary"),
        name="da_qkv",
    )(x, g.reshape(1, d), w, tab)


def _causal_sweep(qi, produce, consume, sa_ref, sb_ref, tk):
    produce(0, sa_ref)

    def pair(u, carry):
        consume(2 * u, sa_ref, None)
        produce(2 * u + 1, sb_ref)
        consume(2 * u + 1, sb_ref, None)
        produce(2 * u + 2, sa_ref)
        return carry

    lax.fori_loop(0, qi, pair, 0)
    produce(2 * qi + 1, sb_ref)
    consume(2 * qi, sa_ref, 0)
    consume(2 * qi + 1, sb_ref, tk)


def _softmax_update(s, diag, m_prev, l_prev):
    tq, tk = s.shape
    if diag is not None:
        row = lax.broadcasted_iota(jnp.int32, (tq, tk), 0)
        col = lax.broadcasted_iota(jnp.int32, (tq, tk), 1)
        s = jnp.where(col + diag <= row, s, NEG_BIG)
    m_next = jnp.maximum(m_prev, jnp.max(s, axis=1, keepdims=True))
    alpha = jnp.exp2(m_prev - m_next)
    p = jnp.exp2(s - jnp.tile(m_next, (1, tk // LANES)))
    l_next = alpha * l_prev + jnp.sum(p, axis=1, keepdims=True)
    return m_next, l_next, alpha, p.astype(BF16)


def _da_attn_kernel(q_ref, k_ref, v_ref, lam_ref, sub_ref, o_ref, sa_ref, sb_ref, m_ref, l_ref, acc_ref,
                    *, tk, lambda_init):
    qi = pl.program_id(2)
    hd = DA_HEAD_DIM
    m_ref[...] = jnp.full(m_ref.shape, NEG_BIG, F32)
    l_ref[...] = jnp.zeros(l_ref.shape, F32)
    acc_ref[...] = jnp.zeros(acc_ref.shape, F32)
    q = q_ref[...]

    def produce(blk, s_ref):
        k = k_ref[pl.ds(pl.multiple_of(blk * tk, tk), tk), :]
        for c in range(2):
            s_ref[c] = lax.dot_general(q[:, c * hd:(c + 1) * hd], k[:, c * hd:(c + 1) * hd],
                                       (((1,), (1,)), ((), ())), preferred_element_type=F32)

    def consume(blk, s_ref, diag):
        v = v_ref[pl.ds(pl.multiple_of(blk * tk, tk), tk), :]
        for c in range(2):
            m_next, l_next, alpha, p = _softmax_update(s_ref[c], diag, m_ref[c], l_ref[c])
            m_ref[c] = m_next
            l_ref[c] = l_next
            pv = jnp.dot(p, v, preferred_element_type=F32)
            acc_ref[c] = acc_ref[c] * jnp.tile(alpha, (1, 2 * hd // LANES)) + pv

    _causal_sweep(qi, produce, consume, sa_ref, sb_ref, tk)

    lv = lam_ref[...]
    lam = (jnp.exp(jnp.sum(lv[0:1] * lv[1:2], axis=1, keepdims=True))
           - jnp.exp(jnp.sum(lv[2:3] * lv[3:4], axis=1, keepdims=True)) + lambda_init)
    rep = 2 * hd // LANES
    o = (acc_ref[0] / jnp.tile(l_ref[0], (1, rep))
         - lam * (acc_ref[1] / jnp.tile(l_ref[1], (1, rep))))
    o_ref[...] = (_rms(o, sub_ref[...]) * (1.0 - lambda_init)).astype(o_ref.dtype)


def _attn_blocks(seq, tk_target=512):
    tk = _pick(seq // 2, tk_target)
    assert seq % (2 * tk) == 0 and tk % LANES == 0
    return 2 * tk, tk


def _da_attention(qkv, lam_vecs, subln, lambda_init, batch, seq):
    d3 = qkv.shape[1]
    d = d3 // 3
    hw = 2 * DA_HEAD_DIM
    heads = d // hw
    tq, tk = _attn_blocks(seq)
    qkv3 = qkv.reshape(batch, seq, d3)
    kern = functools.partial(_da_attn_kernel, tk=tk, lambda_init=lambda_init)
    resident = pl.Buffered(1)
    out = pl.pallas_call(
        kern,
        grid=(batch, heads, seq // tq),
        in_specs=[pl.BlockSpec((None, tq, hw), lambda b, h, i: (b, i, h)),
                  pl.BlockSpec((None, seq, hw), lambda b, h, i: (b, 0, heads + h), pipeline_mode=resident),
                  pl.BlockSpec((None, seq, hw), lambda b, h, i: (b, 0, 2 * heads + h), pipeline_mode=resident),
                  pl.BlockSpec((4, DA_HEAD_DIM), lambda b, h, i: (0, 0)),
                  pl.BlockSpec((1, hw), lambda b, h, i: (0, 0))],
        out_specs=pl.BlockSpec((None, tq, hw), lambda b, h, i: (b, i, h)),
        out_shape=jax.ShapeDtypeStruct((batch, seq, d), BF16),
        scratch_shapes=[pltpu.VMEM((2, tq, tk), F32),
                        pltpu.VMEM((2, tq, tk), F32),
                        pltpu.VMEM((2, tq, LANES), F32),
                        pltpu.VMEM((2, tq, LANES), F32),
                        pltpu.VMEM((2, tq, hw), F32)],
        compiler_params=_cparams("parallel", "parallel", "arbitrary"),
        name="da_attn",
    )(qkv3, qkv3, qkv3, lam_vecs, subln.reshape(1, hw))
    return out.reshape(batch * seq, d)


def _mla_proj_kernel(x_ref, g_ref, wd_ref, qn_ref, kvn_ref, wuq_ref, wukv_ref, tab_ref,
                     q_ref, kv_ref, kr_ref, *, q_lora, kv_lora, heads, q_scale):
    h = _rms(x_ref[...], g_ref[...]).astype(BF16)
    d = jnp.dot(h, wd_ref[...], preferred_element_type=F32)
    tab = tab_ref[...]
    c_q = _rms(d[:, :q_lora], qn_ref[...]).astype(BF16)
    c_kv = _rms(d[:, q_lora:q_lora + kv_lora], kvn_ref[...]).astype(BF16)
    kr = _rope_chunk(d[:, q_lora + kv_lora:q_lora + kv_lora + LANES], tab, MLA_ROPE // 2)
    kr_ref[...] = kr.astype(kr_ref.dtype)
    w = MLA_QK_PAD
    for hh in range(heads):
        qh = jnp.dot(c_q, wuq_ref[:, hh * w:(hh + 1) * w], preferred_element_type=F32)
        q_ref[:, hh * w:hh * w + LANES] = (qh[:, :LANES] * q_scale).astype(q_ref.dtype)
        qr = _rope_chunk(qh[:, LANES:], tab, MLA_ROPE // 2) * q_scale
        q_ref[:, hh * w + LANES:(hh + 1) * w] = qr.astype(q_ref.dtype)
        kv_ref[:, hh * w:(hh + 1) * w] = jnp.dot(
            c_kv, wukv_ref[:, hh * w:(hh + 1) * w], preferred_element_type=F32).astype(kv_ref.dtype)


def _mla_proj(x, g, wd_pad, qn, kvn, wuq_pad, wukv, tab, heads, tm_target=256):
    m, d = x.shape
    tm = _pick(m, tm_target)
    q_lora, kv_lora = qn.shape[0], kvn.shape[0]
    q_scale = LOG2E / math.sqrt(MLA_NOPE + MLA_ROPE)
    kern = functools.partial(_mla_proj_kernel, q_lora=q_lora, kv_lora=kv_lora, heads=heads, q_scale=q_scale)
    const = lambda i: (0, 0)
    row = lambda i: (i, 0)
    nq, nkv = wuq_pad.shape[1], wukv.shape[1]
    return pl.pallas_call(
        kern,
        grid=(m // tm,),
        in_specs=[pl.BlockSpec((tm, d), row),
                  pl.BlockSpec((1, d), const),
                  pl.BlockSpec(wd_pad.shape, const),
                  pl.BlockSpec((1, q_lora), const),
                  pl.BlockSpec((1, kv_lora), const),
                  pl.BlockSpec(wuq_pad.shape, const),
                  pl.BlockSpec(wukv.shape, const),
                  pl.BlockSpec((tm, 3 * LANES), row)],
        out_specs=[pl.BlockSpec((tm, nq), row),
                   pl.BlockSpec((tm, nkv), row),
                   pl.BlockSpec((tm, LANES), row)],
        out_shape=[jax.ShapeDtypeStruct((m, nq), BF16),
                   jax.ShapeDtypeStruct((m, nkv), BF16),
                   jax.ShapeDtypeStruct((m, LANES), BF16)],
        compiler_params=_cparams("parallel"),
        name="mla_proj",
    )(x, g.reshape(1, d), wd_pad, qn.reshape(1, q_lora), kvn.reshape(1, kv_lora), wuq_pad, wukv, tab)


def _mla_attn_kernel(q_ref, kn_ref, kr_ref, v_ref, o_ref, sa_ref, sb_ref, m_ref, l_ref, acc_ref, *, tk):
    qi = pl.program_id(2)
    m_ref[...] = jnp.full(m_ref.shape, NEG_BIG, F32)
    l_ref[...] = jnp.zeros(l_ref.shape, F32)
    acc_ref[...] = jnp.zeros(acc_ref.shape, F32)
    q = q_ref[...]

    def produce(blk, s_ref):
        rows = pl.ds(pl.multiple_of(blk * tk, tk), tk)
        k = jnp.concatenate([kn_ref[rows, :], kr_ref[rows, :]], axis=1)
        s_ref[...] = lax.dot_general(q, k, (((1,), (1,)), ((), ())), preferred_element_type=F32)

    def consume(blk, s_ref, diag):
        v = v_ref[pl.ds(pl.multiple_of(blk * tk, tk), tk), :]
        m_next, l_next, alpha, p = _softmax_update(s_ref[...], diag, m_ref[...], l_ref[...])
        m_ref[...] = m_next
        l_ref[...] = l_next
        acc_ref[...] = acc_ref[...] * alpha + jnp.dot(p, v, preferred_element_type=F32)

    _causal_sweep(qi, produce, consume, sa_ref, sb_ref, tk)
    o_ref[...] = (acc_ref[...] / l_ref[...]).astype(o_ref.dtype)


def _mla_attention(q, kv, kr, batch, seq, heads):
    tq, tk = _attn_blocks(seq)
    w = MLA_QK_PAD
    q3 = q.reshape(batch, seq, heads * w)
    kv3 = kv.reshape(batch, seq, heads * w)
    kr3 = kr.reshape(batch, seq, LANES)
    kern = functools.partial(_mla_attn_kernel, tk=tk)
    out = pl.pallas_call(
        kern,
        grid=(batch, heads, seq // tq),
        in_specs=[pl.BlockSpec((None, tq, w), lambda b, h, i: (b, i, h)),
                  pl.BlockSpec((None, seq, LANES), lambda b, h, i: (b, 0, 2 * h)),
                  pl.BlockSpec((None, seq, LANES), lambda b, h, i: (b, 0, 0)),
                  pl.BlockSpec((None, seq, LANES), lambda b, h, i: (b, 0, 2 * h + 1))],
        out_specs=pl.BlockSpec((None, tq, MLA_V), lambda b, h, i: (b, i, h)),
        out_shape=jax.ShapeDtypeStruct((batch, seq, heads * MLA_V), BF16),
        scratch_shapes=[pltpu.VMEM((tq, tk), F32),
                        pltpu.VMEM((tq, tk), F32),
                        pltpu.VMEM((tq, LANES), F32),
                        pltpu.VMEM((tq, LANES), F32),
                        pltpu.VMEM((tq, MLA_V), F32)],
        compiler_params=_cparams("parallel", "parallel", "arbitrary"),
        name="mla_attn",
    )(q3, kv3, kr3, kv3)
    return out.reshape(batch * seq, heads * MLA_V)


def _proj_residual_kernel(o_ref, w_ref, x_ref, out_ref):
    out_ref[...] = x_ref[...] + jnp.dot(o_ref[...], w_ref[...], preferred_element_type=F32)


def _proj_residual(o, w, x, tm_target=512, tn_target=1024):
    m, kdim = o.shape
    n = w.shape[1]
    tm, tn = _pick(m, tm_target), _pick(n, tn_target)
    return pl.pallas_call(
        _proj_residual_kernel,
        grid=(m // tm, n // tn),
        in_specs=[pl.BlockSpec((tm, kdim), lambda i, j: (i, 0)),
                  pl.BlockSpec((kdim, tn), lambda i, j: (0, j)),
                  pl.BlockSpec((tm, tn), lambda i, j: (i, j))],
        out_specs=pl.BlockSpec((tm, tn), lambda i, j: (i, j)),
        out_shape=jax.ShapeDtypeStruct((m, n), F32),
        compiler_params=_cparams("parallel", "parallel"),
        name="proj_residual",
    )(o, w, x)


def _cross_attn_kernel(x_ref, g_ref, wq_ref, kv_ref, wo_ref, out_ref, *, q_scale):
    x = x_ref[...]
    h = _rms(x, g_ref[...]).astype(BF16)
    q = (jnp.dot(h, wq_ref[...], preferred_element_type=F32) * q_scale).astype(BF16)
    hd = CA_HEAD_DIM
    kw = CA_HEADS * hd
    outs = []
    for hh in range(CA_HEADS):
        k = kv_ref[:, hh * hd:(hh + 1) * hd]
        v = kv_ref[:, kw + hh * hd:kw + (hh + 1) * hd]
        s = lax.dot_general(q[:, hh * hd:(hh + 1) * hd], k, (((1,), (1,)), ((), ())),
                            preferred_element_type=F32)
        p = jnp.exp2(s - jnp.max(s, axis=1, keepdims=True))
        l = jnp.sum(p, axis=1, keepdims=True)
        outs.append((jnp.dot(p.astype(BF16), v, preferred_element_type=F32) / l).astype(BF16))
    o = jnp.concatenate(outs, axis=1)
    out_ref[...] = x + jnp.dot(o, wo_ref[...], preferred_element_type=F32)


def _cross_attn(x, g, wq, kv_all, layer, wo, seq, tm_target=512):
    m, d = x.shape
    tm = _pick(seq, tm_target)
    blocks_per_batch = seq // tm
    n_mem = kv_all.shape[1]
    kvw = 2 * CA_HEADS * CA_HEAD_DIM
    kern = functools.partial(_cross_attn_kernel, q_scale=LOG2E / math.sqrt(CA_HEAD_DIM))
    return pl.pallas_call(
        kern,
        grid=(m // tm,),
        in_specs=[pl.BlockSpec((tm, d), lambda i: (i, 0)),
                  pl.BlockSpec((1, d), lambda i: (0, 0)),
                  pl.BlockSpec(wq.shape, lambda i: (0, 0)),
                  pl.BlockSpec((None, n_mem, kvw), lambda i: (i // blocks_per_batch, 0, layer)),
                  pl.BlockSpec(wo.shape, lambda i: (0, 0))],
        out_specs=pl.BlockSpec((tm, d), lambda i: (i, 0)),
        out_shape=jax.ShapeDtypeStruct((m, d), F32),
        compiler_params=_cparams("parallel"),
        name="cross_attn",
    )(x, g.reshape(1, d), wq, kv_all, wo)


def _mlp_kernel(x_ref, g_ref, wup_ref, wdown_ref, fg_ref, out_ref, h_ref, *, final_norm):
    f = pl.program_id(1)

    @pl.when(f == 0)
    def _():
        x = x_ref[...]
        h_ref[...] = _rms(x, g_ref[...]).astype(BF16)
        out_ref[...] = x

    u = jnp.maximum(jnp.dot(h_ref[...], wup_ref[...], preferred_element_type=F32), 0.0)
    out_ref[...] += jnp.dot((u * u).astype(BF16), wdown_ref[...], preferred_element_type=F32)

    if final_norm:
        @pl.when(f == pl.num_programs(1) - 1)
        def _():
            out_ref[...] = _rms(out_ref[...], fg_ref[...])


def _mlp(x, g, wup, wdown, final_g, final_norm, tm_target=1024, tf_target=512):
    m, d = x.shape
    ff = wup.shape[1]
    tm, tf = _pick(m, tm_target), _pick(ff, tf_target)
    kern = functools.partial(_mlp_kernel, final_norm=final_norm)
    return pl.pallas_call(
        kern,
        grid=(m // tm, ff // tf),
        in_specs=[pl.BlockSpec((tm, d), lambda i, f: (i, 0)),
                  pl.BlockSpec((1, d), lambda i, f: (0, 0)),
                  pl.BlockSpec((d, tf), lambda i, f: (0, f)),
                  pl.BlockSpec((tf, d), lambda i, f: (f, 0)),
                  pl.BlockSpec((1, d), lambda i, f: (0, 0))],
        out_specs=pl.BlockSpec((tm, d), lambda i, f: (i, 0)),
        out_shape=jax.ShapeDtypeStruct((m, d), F32),
        scratch_shapes=[pltpu.VMEM((tm, d), BF16)],
        compiler_params=_cparams("parallel", "arbitrary"),
        name="mlp",
    )(x, g.reshape(1, d), wup, wdown, final_g.reshape(1, d))


def kernel(x, mem, positions, attn_norm, cross_norm, mlp_norm, mem_norm, final_norm, da_wqkv, da_lambda, da_subln, da_wo, mla_wdown, mla_q_norm, mla_kv_norm, mla_wuq, mla_wukv, mla_wo, ca_wq, ca_wkv, ca_wo, mlp_wup, mlp_wdown):
    batch, seq, d = x.shape
    depth = attn_norm.shape[0]
    n_mem = mem.shape[1]
    mla_heads = mla_wo.shape[1] // MLA_V

    tab_da = _rope_table(positions, DA_ROT)
    tab_mla = _rope_table(positions, MLA_ROPE)

    wkv_all = jnp.concatenate([ca_wkv[i] for i in range(depth)], axis=1).astype(BF16)
    kv_all = _norm_matmul(mem.reshape(batch * n_mem, d), mem_norm, wkv_all, BF16)
    kv_all = kv_all.reshape(batch, n_mem, wkv_all.shape[1])

    xs = x.reshape(batch * seq, d)
    for i in range(depth):
        j = i // 2
        if i % 2 == 0:
            lambda_init = 0.8 - 0.6 * math.exp(-0.3 * i)
            qkv = _da_qkv(xs, attn_norm[i], da_wqkv[j].astype(BF16), tab_da)
            o = _da_attention(qkv, da_lambda[j], da_subln[j], lambda_init, batch, seq)
            xs = _proj_residual(o, da_wo[j].astype(BF16), xs)
        else:
            q_lora = mla_q_norm.shape[1]
            wd = mla_wdown[j]
            wd_pad = jnp.pad(wd, ((0, 0), (0, LANES - MLA_ROPE))).astype(BF16)
            wuq = mla_wuq[j].reshape(q_lora, mla_heads, MLA_NOPE + MLA_ROPE)
            wuq_pad = jnp.pad(wuq, ((0, 0), (0, 0), (0, MLA_QK_PAD - MLA_NOPE - MLA_ROPE)))
            wuq_pad = wuq_pad.reshape(q_lora, mla_heads * MLA_QK_PAD).astype(BF16)
            q, kv, kr = _mla_proj(xs, attn_norm[i], wd_pad, mla_q_norm[j], mla_kv_norm[j], wuq_pad,
                                  mla_wukv[j].astype(BF16), tab_mla, mla_heads)
            o = _mla_attention(q, kv, kr, batch, seq, mla_heads)
            xs = _proj_residual(o, mla_wo[j].astype(BF16), xs)
        xs = _cross_attn(xs, cross_norm[i], ca_wq[i].astype(BF16), kv_all, i, ca_wo[i].astype(BF16), seq)
        xs = _mlp(xs, mlp_norm[i], mlp_wup[i].astype(BF16), mlp_wdown[i].astype(BF16), final_norm,
                  final_norm=(i == depth - 1))
    return xs.reshape(batch, seq, d)
```

```python
import functools
import math

import jax
import jax.numpy as jnp
from jax import lax
from jax.experimental import pallas as pl
from jax.experimental.pallas import tpu as pltpu

F32 = jnp.float32
BF16 = jnp.bfloat16

EPS = 1e-6
ROPE_THETA = 500000.0
LANES = 128
LOG2E = 1.4426950408889634
NEG_BIG = -1e30

DA_HEAD_DIM = 128
DA_ROT = DA_HEAD_DIM // 4
MLA_NOPE = 128
MLA_ROPE = 64
MLA_V = 128
MLA_QK_PAD = 256
CA_HEADS = 4
CA_HEAD_DIM = 128

VMEM_LIMIT = 56 * 1024 * 1024


def _cparams(*sem):
    return pltpu.CompilerParams(dimension_semantics=sem, vmem_limit_bytes=VMEM_LIMIT)


def _pick(n, target):
    t = min(n, target)
    while n % t or (t % 8 and t != n):
        t -= 1
    return t


def _rms(xf, g):
    ms = jnp.mean(xf * xf, axis=-1, keepdims=True)
    return xf * lax.rsqrt(ms + EPS) * g


HALF_LANES = LANES // 2


def _rope_chunk(xc, c, s):
    return xc * c + pltpu.roll(xc, HALF_LANES, 1) * s


def _rope_lane_perm(rot):
    half = rot // 2
    perm = list(range(LANES))
    perm[half:rot], perm[HALF_LANES:HALF_LANES + half] = perm[HALF_LANES:HALF_LANES + half], perm[half:rot]
    return jnp.array(perm, jnp.int32)


def _rope_table(positions, rot):
    half = rot // 2
    inv_freq = ROPE_THETA ** (-jnp.arange(0, rot, 2, dtype=F32) / rot)
    ang = positions.astype(F32).reshape(-1)[:, None] * inv_freq
    cos, sin = jnp.cos(ang), jnp.sin(ang)
    n = ang.shape[0]
    gap = HALF_LANES - half
    c = jnp.concatenate([cos, jnp.ones((n, gap), F32), cos, jnp.ones((n, gap), F32)], axis=1)
    s = jnp.concatenate([-sin, jnp.zeros((n, gap), F32), sin, jnp.zeros((n, gap), F32)], axis=1)
    return jnp.concatenate([c, s], axis=1)


def _norm_matmul_kernel(x_ref, g_ref, w_ref, o_ref, h_ref):
    @pl.when(pl.program_id(1) == 0)
    def _():
        h_ref[...] = _rms(x_ref[...], g_ref[...]).astype(BF16)

    o_ref[...] = jnp.dot(h_ref[...], w_ref[...], preferred_element_type=F32).astype(o_ref.dtype)


def _norm_matmul(x, g, w, out_dtype, tm_target=512, tn_target=1024):
    m, d = x.shape
    n = w.shape[1]
    tm, tn = _pick(m, tm_target), _pick(n, tn_target)
    return pl.pallas_call(
        _norm_matmul_kernel,
        grid=(m // tm, n // tn),
        in_specs=[pl.BlockSpec((tm, d), lambda i, j: (i, 0)),
                  pl.BlockSpec((1, d), lambda i, j: (0, 0)),
                  pl.BlockSpec((d, tn), lambda i, j: (0, j))],
        out_specs=pl.BlockSpec((tm, tn), lambda i, j: (i, j)),
        out_shape=jax.ShapeDtypeStruct((m, n), out_dtype),
        scratch_shapes=[pltpu.VMEM((tm, d), BF16)],
        compiler_params=_cparams("parallel", "arbitrary"),
        name="norm_matmul",
    )(x, g.reshape(1, d), w)


def _da_qkv_kernel(x_ref, g_ref, w_ref, tab_ref, o_ref, h_ref, *, n_head_blocks, q_scale):
    j = pl.program_id(1)

    @pl.when(j == 0)
    def _():
        h_ref[...] = _rms(x_ref[...], g_ref[...]).astype(BF16)

    acc = jnp.dot(h_ref[...], w_ref[...], preferred_element_type=F32)
    rotary = j < 2 * n_head_blocks
    scale = jnp.where(j < n_head_blocks, q_scale, 1.0).astype(F32)
    c = jnp.where(rotary, tab_ref[:, 0:LANES], 1.0) * scale
    s = jnp.where(rotary, tab_ref[:, LANES:2 * LANES], 0.0) * scale
    for ch in range(acc.shape[1] // LANES):
        cols = slice(ch * LANES, (ch + 1) * LANES)
        o_ref[:, cols] = _rope_chunk(acc[:, cols], c, s).astype(o_ref.dtype)


def _da_qkv(x, g, w, tab, tm_target=512, tn_target=1024):
    m, d = x.shape
    n = w.shape[1]
    tm, tn = _pick(m, tm_target), _pick(d, tn_target)
    q_scale = LOG2E / math.sqrt(DA_HEAD_DIM)
    kern = functools.partial(_da_qkv_kernel, n_head_blocks=d // tn, q_scale=q_scale)
    return pl.pallas_call(
        kern,
        grid=(m // tm, n // tn),
        in_specs=[pl.BlockSpec((tm, d), lambda i, j: (i, 0)),
                  pl.BlockSpec((1, d), lambda i, j: (0, 0)),
                  pl.BlockSpec((d, tn), lambda i, j: (0, j)),
                  pl.BlockSpec((tm, 2 * LANES), lambda i, j: (i, 0))],
        out_specs=pl.BlockSpec((tm, tn), lambda i, j: (i, j)),
        out_shape=jax.ShapeDtypeStruct((m, n), BF16),
        scratch_shapes=[pltpu.VMEM((tm, d), BF16)],
        compiler_params=_cparams("parallel", "arbitrary"),
        name="da_qkv",
    )(x, g.reshape(1, d), w, tab)


def _causal_sweep(qi, tk, score, softmax, accumulate):
    score(2 * qi, 0)
    softmax(0, 0)
    score(2 * qi + 1, 1)
    accumulate(2 * qi, 0)
    softmax(1, tk)
    accumulate(2 * qi + 1, 1)

    @pl.when(qi > 0)
    def _():
        score(0, 0)
        softmax(0, None)
        score(1, 1)

        def pair(u, carry):
            t = 2 * u
            accumulate(t, 0)
            softmax(1, None)
            score(t + 2, 0)
            accumulate(t + 1, 1)
            softmax(0, None)
            score(t + 3, 1)
            return carry

        lax.fori_loop(0, qi - 1, pair, 0)
        t = 2 * qi - 2
        accumulate(t, 0)
        softmax(1, None)
        accumulate(t + 1, 1)


def _softmax_update(s, diag, m_prev, l_prev):
    tq, tk = s.shape
    if diag is not None:
        row = lax.broadcasted_iota(jnp.int32, (tq, tk), 0)
        col = lax.broadcasted_iota(jnp.int32, (tq, tk), 1)
        s = jnp.where(col + diag <= row, s, NEG_BIG)
    m_next = jnp.maximum(m_prev, jnp.max(s, axis=1, keepdims=True))
    alpha = jnp.exp2(m_prev - m_next)
    p = jnp.exp2(s - jnp.tile(m_next, (1, tk // LANES)))
    part = p[:, 0:LANES]
    for ch in range(1, tk // LANES):
        part = part + p[:, ch * LANES:(ch + 1) * LANES]
    return m_next, alpha * l_prev + part, alpha, p.astype(BF16)


def _attn_stages(n_streams, tk, q_of, k_of, v_of, s_ref, p_ref, a_ref, m_ref, l_ref, acc_ref):
    nt = (((1,), (1,)), ((), ()))

    def score(blk, par):
        rows = pl.ds(pl.multiple_of(blk * tk, tk), tk)
        for c in range(n_streams):
            s_ref[par, c] = lax.dot_general(q_of(c), k_of(c, rows), nt, preferred_element_type=F32)

    def softmax(par, diag):
        for c in range(n_streams):
            m_next, l_next, alpha, p = _softmax_update(s_ref[par, c], diag, m_ref[c], l_ref[c])
            m_ref[c] = m_next
            l_ref[c] = l_next
            a_ref[par, c] = alpha
            p_ref[par, c] = p

    def accumulate(blk, par):
        v = v_of(pl.ds(pl.multiple_of(blk * tk, tk), tk))
        rep = acc_ref.shape[-1] // LANES
        for c in range(n_streams):
            pv = jnp.dot(p_ref[par, c], v, preferred_element_type=F32)
            acc_ref[c] = acc_ref[c] * jnp.tile(a_ref[par, c], (1, rep)) + pv

    return score, softmax, accumulate


def _attn_scratch(n_streams, tq, tk, vw):
    return [pltpu.VMEM((2, n_streams, tq, tk), F32),
            pltpu.VMEM((2, n_streams, tq, tk), BF16),
            pltpu.VMEM((2, n_streams, tq, LANES), F32),
            pltpu.VMEM((n_streams, tq, LANES), F32),
            pltpu.VMEM((n_streams, tq, LANES), F32),
            pltpu.VMEM((n_streams, tq, vw), F32)]


def _da_attn_kernel(q_ref, k_ref, v_ref, lam_ref, sub_ref, o_ref, s_ref, p_ref, a_ref, m_ref, l_ref, acc_ref,
                    *, tk, lambda_init):
    qi = pl.program_id(2)
    hd = DA_HEAD_DIM
    m_ref[...] = jnp.full(m_ref.shape, NEG_BIG, F32)
    l_ref[...] = jnp.zeros(l_ref.shape, F32)
    acc_ref[...] = jnp.zeros(acc_ref.shape, F32)

    stages = _attn_stages(
        2, tk,
        lambda c: q_ref[:, c * hd:(c + 1) * hd],
        lambda c, rows: k_ref[rows, c * hd:(c + 1) * hd],
        lambda rows: v_ref[rows, :],
        s_ref, p_ref, a_ref, m_ref, l_ref, acc_ref)
    _causal_sweep(qi, tk, *stages)

    lv = lam_ref[...]
    lam = (jnp.exp(jnp.sum(lv[0:1] * lv[1:2], axis=1, keepdims=True))
           - jnp.exp(jnp.sum(lv[2:3] * lv[3:4], axis=1, keepdims=True)) + lambda_init)
    l0 = jnp.sum(l_ref[0], axis=1, keepdims=True)
    l1 = jnp.sum(l_ref[1], axis=1, keepdims=True)
    o = acc_ref[0] / l0 - lam * (acc_ref[1] / l1)
    o_ref[...] = (_rms(o, sub_ref[...]) * (1.0 - lambda_init)).astype(o_ref.dtype)


def _attn_blocks(seq, tk_target=512):
    tk = _pick(seq // 2, tk_target)
    assert seq % (2 * tk) == 0 and tk % LANES == 0
    return 2 * tk, tk


def _da_attention(qkv, lam_vecs, subln, lambda_init, batch, seq):
    d3 = qkv.shape[1]
    d = d3 // 3
    hw = 2 * DA_HEAD_DIM
    heads = d // hw
    tq, tk = _attn_blocks(seq)
    qkv3 = qkv.reshape(batch, seq, d3)
    kern = functools.partial(_da_attn_kernel, tk=tk, lambda_init=lambda_init)
    resident = pl.Buffered(1)
    out = pl.pallas_call(
        kern,
        grid=(batch, heads, seq // tq),
        in_specs=[pl.BlockSpec((None, tq, hw), lambda b, h, i: (b, i, h)),
                  pl.BlockSpec((None, seq, hw), lambda b, h, i: (b, 0, heads + h), pipeline_mode=resident),
                  pl.BlockSpec((None, seq, hw), lambda b, h, i: (b, 0, 2 * heads + h), pipeline_mode=resident),
                  pl.BlockSpec((4, DA_HEAD_DIM), lambda b, h, i: (0, 0)),
                  pl.BlockSpec((1, hw), lambda b, h, i: (0, 0))],
        out_specs=pl.BlockSpec((None, tq, hw), lambda b, h, i: (b, i, h)),
        out_shape=jax.ShapeDtypeStruct((batch, seq, d), BF16),
        scratch_shapes=_attn_scratch(2, tq, tk, hw),
        compiler_params=_cparams("parallel", "parallel", "arbitrary"),
        name="da_attn",
    )(qkv3, qkv3, qkv3, lam_vecs, subln.reshape(1, hw))
    return out.reshape(batch * seq, d)


def _mla_proj_kernel(x_ref, g_ref, wd_ref, qn_ref, kvn_ref, wuq_ref, wukv_ref, tab_ref,
                     q_ref, kv_ref, kr_ref, *, q_lora, kv_lora, heads, q_scale):
    h = _rms(x_ref[...], g_ref[...]).astype(BF16)
    d = jnp.dot(h, wd_ref[...], preferred_element_type=F32)
    rc = tab_ref[:, 0:LANES]
    rs = tab_ref[:, LANES:2 * LANES]
    c_q = _rms(d[:, :q_lora], qn_ref[...]).astype(BF16)
    c_kv = _rms(d[:, q_lora:q_lora + kv_lora], kvn_ref[...]).astype(BF16)
    kr = _rope_chunk(d[:, q_lora + kv_lora:q_lora + kv_lora + LANES], rc, rs)
    kr_ref[...] = kr.astype(kr_ref.dtype)
    w = MLA_QK_PAD
    qc, qs = rc * q_scale, rs * q_scale
    for hh in range(heads):
        qh = jnp.dot(c_q, wuq_ref[:, hh * w:(hh + 1) * w], preferred_element_type=F32)
        q_ref[:, hh * w:hh * w + LANES] = (qh[:, :LANES] * q_scale).astype(q_ref.dtype)
        q_ref[:, hh * w + LANES:(hh + 1) * w] = _rope_chunk(qh[:, LANES:], qc, qs).astype(q_ref.dtype)
        kv_ref[:, hh * w:(hh + 1) * w] = jnp.dot(
            c_kv, wukv_ref[:, hh * w:(hh + 1) * w], preferred_element_type=F32).astype(kv_ref.dtype)


def _mla_proj(x, g, wd_pad, qn, kvn, wuq_pad, wukv, tab, heads, tm_target=256):
    m, d = x.shape
    tm = _pick(m, tm_target)
    q_lora, kv_lora = qn.shape[0], kvn.shape[0]
    q_scale = LOG2E / math.sqrt(MLA_NOPE + MLA_ROPE)
    kern = functools.partial(_mla_proj_kernel, q_lora=q_lora, kv_lora=kv_lora, heads=heads, q_scale=q_scale)
    const = lambda i: (0, 0)
    row = lambda i: (i, 0)
    nq, nkv = wuq_pad.shape[1], wukv.shape[1]
    return pl.pallas_call(
        kern,
        grid=(m // tm,),
        in_specs=[pl.BlockSpec((tm, d), row),
                  pl.BlockSpec((1, d), const),
                  pl.BlockSpec(wd_pad.shape, const),
                  pl.BlockSpec((1, q_lora), const),
                  pl.BlockSpec((1, kv_lora), const),
                  pl.BlockSpec(wuq_pad.shape, const),
                  pl.BlockSpec(wukv.shape, const),
                  pl.BlockSpec((tm, 2 * LANES), row)],
        out_specs=[pl.BlockSpec((tm, nq), row),
                   pl.BlockSpec((tm, nkv), row),
                   pl.BlockSpec((tm, LANES), row)],
        out_shape=[jax.ShapeDtypeStruct((m, nq), BF16),
                   jax.ShapeDtypeStruct((m, nkv), BF16),
                   jax.ShapeDtypeStruct((m, LANES), BF16)],
        compiler_params=_cparams("parallel"),
        name="mla_proj",
    )(x, g.reshape(1, d), wd_pad, qn.reshape(1, q_lora), kvn.reshape(1, kv_lora), wuq_pad, wukv, tab)


def _mla_attn_kernel(q_ref, kn_ref, kr_ref, v_ref, o_ref, s_ref, p_ref, a_ref, m_ref, l_ref, acc_ref, *, tk):
    qi = pl.program_id(2)
    m_ref[...] = jnp.full(m_ref.shape, NEG_BIG, F32)
    l_ref[...] = jnp.zeros(l_ref.shape, F32)
    acc_ref[...] = jnp.zeros(acc_ref.shape, F32)

    stages = _attn_stages(
        1, tk,
        lambda c: q_ref[...],
        lambda c, rows: jnp.concatenate([kn_ref[rows, :], kr_ref[rows, :]], axis=1),
        lambda rows: v_ref[rows, :],
        s_ref, p_ref, a_ref, m_ref, l_ref, acc_ref)
    _causal_sweep(qi, tk, *stages)
    o_ref[...] = (acc_ref[0] / jnp.sum(l_ref[0], axis=1, keepdims=True)).astype(o_ref.dtype)


def _mla_attention(q, kv, kr, batch, seq, heads):
    tq, tk = _attn_blocks(seq)
    w = MLA_QK_PAD
    q3 = q.reshape(batch, seq, heads * w)
    kv3 = kv.reshape(batch, seq, heads * w)
    kr3 = kr.reshape(batch, seq, LANES)
    kern = functools.partial(_mla_attn_kernel, tk=tk)
    out = pl.pallas_call(
        kern,
        grid=(batch, heads, seq // tq),
        in_specs=[pl.BlockSpec((None, tq, w), lambda b, h, i: (b, i, h)),
                  pl.BlockSpec((None, seq, LANES), lambda b, h, i: (b, 0, 2 * h)),
                  pl.BlockSpec((None, seq, LANES), lambda b, h, i: (b, 0, 0)),
                  pl.BlockSpec((None, seq, LANES), lambda b, h, i: (b, 0, 2 * h + 1))],
        out_specs=pl.BlockSpec((None, tq, MLA_V), lambda b, h, i: (b, i, h)),
        out_shape=jax.ShapeDtypeStruct((batch, seq, heads * MLA_V), BF16),
        scratch_shapes=_attn_scratch(1, tq, tk, MLA_V),
        compiler_params=_cparams("parallel", "parallel", "arbitrary"),
        name="mla_attn",
    )(q3, kv3, kr3, kv3)
    return out.reshape(batch * seq, heads * MLA_V)


def _post_attn_kernel(o_ref, wao_ref, x_ref, g_ref, wq_ref, kv_ref, wo_ref, out_ref, *, q_scale):
    x = x_ref[...] + jnp.dot(o_ref[...], wao_ref[...], preferred_element_type=F32)
    h = _rms(x, g_ref[...]).astype(BF16)
    q = (jnp.dot(h, wq_ref[...], preferred_element_type=F32) * q_scale).astype(BF16)
    hd = CA_HEAD_DIM
    kw = CA_HEADS * hd
    outs = []
    for hh in range(CA_HEADS):
        k = kv_ref[:, hh * hd:(hh + 1) * hd]
        v = kv_ref[:, kw + hh * hd:kw + (hh + 1) * hd]
        s = lax.dot_general(q[:, hh * hd:(hh + 1) * hd], k, (((1,), (1,)), ((), ())),
                            preferred_element_type=F32)
        p = jnp.exp2(s - jnp.max(s, axis=1, keepdims=True))
        l = jnp.sum(p, axis=1, keepdims=True)
        outs.append((jnp.dot(p.astype(BF16), v, preferred_element_type=F32) / l).astype(BF16))
    o = jnp.concatenate(outs, axis=1)
    out_ref[...] = x + jnp.dot(o, wo_ref[...], preferred_element_type=F32)


def _post_attn(o, wao, x, g, wq, kv_all, layer, wo, seq, tm_target=512):
    m, d = x.shape
    tm = _pick(seq, tm_target)
    blocks_per_batch = seq // tm
    n_mem = kv_all.shape[1]
    kvw = 2 * CA_HEADS * CA_HEAD_DIM
    kern = functools.partial(_post_attn_kernel, q_scale=LOG2E / math.sqrt(CA_HEAD_DIM))
    const = lambda i: (0, 0)
    once = pl.Buffered(1)
    return pl.pallas_call(
        kern,
        grid=(m // tm,),
        in_specs=[pl.BlockSpec((tm, o.shape[1]), lambda i: (i, 0)),
                  pl.BlockSpec(wao.shape, const, pipeline_mode=once),
                  pl.BlockSpec((tm, d), lambda i: (i, 0)),
                  pl.BlockSpec((1, d), const),
                  pl.BlockSpec(wq.shape, const, pipeline_mode=once),
                  pl.BlockSpec((None, n_mem, kvw), lambda i: (i // blocks_per_batch, 0, layer)),
                  pl.BlockSpec(wo.shape, const, pipeline_mode=once)],
        out_specs=pl.BlockSpec((tm, d), lambda i: (i, 0)),
        out_shape=jax.ShapeDtypeStruct((m, d), F32),
        compiler_params=_cparams("parallel"),
        name="post_attn",
    )(o, wao, x, g.reshape(1, d), wq, kv_all, wo)


def _mlp_kernel(x_ref, g_ref, wup_ref, wdown_ref, fg_ref, out_ref, h_ref, *, final_norm):
    f = pl.program_id(1)

    @pl.when(f == 0)
    def _():
        x = x_ref[...]
        h_ref[...] = _rms(x, g_ref[...]).astype(BF16)
        out_ref[...] = x

    u = jnp.maximum(jnp.dot(h_ref[...], wup_ref[...], preferred_element_type=F32), 0.0)
    out_ref[...] += jnp.dot((u * u).astype(BF16), wdown_ref[...], preferred_element_type=F32)

    if final_norm:
        @pl.when(f == pl.num_programs(1) - 1)
        def _():
            out_ref[...] = _rms(out_ref[...], fg_ref[...])


def _mlp(x, g, wup, wdown, final_g, final_norm, tm_target=1024, tf_target=512):
    m, d = x.shape
    ff = wup.shape[1]
    tm, tf = _pick(m, tm_target), _pick(ff, tf_target)
    kern = functools.partial(_mlp_kernel, final_norm=final_norm)
    return pl.pallas_call(
        kern,
        grid=(m // tm, ff // tf),
        in_specs=[pl.BlockSpec((tm, d), lambda i, f: (i, 0)),
                  pl.BlockSpec((1, d), lambda i, f: (0, 0)),
                  pl.BlockSpec((d, tf), lambda i, f: (0, f)),
                  pl.BlockSpec((tf, d), lambda i, f: (f, 0)),
                  pl.BlockSpec((1, d), lambda i, f: (0, 0))],
        out_specs=pl.BlockSpec((tm, d), lambda i, f: (i, 0)),
        out_shape=jax.ShapeDtypeStruct((m, d), F32),
        scratch_shapes=[pltpu.VMEM((tm, d), BF16)],
        compiler_params=_cparams("parallel", "arbitrary"),
        name="mlp",
    )(x, g.reshape(1, d), wup, wdown, final_g.reshape(1, d))


def kernel(x, mem, positions, attn_norm, cross_norm, mlp_norm, mem_norm, final_norm, da_wqkv, da_lambda, da_subln, da_wo, mla_wdown, mla_q_norm, mla_kv_norm, mla_wuq, mla_wukv, mla_wo, ca_wq, ca_wkv, ca_wo, mlp_wup, mlp_wdown):
    batch, seq, d = x.shape
    depth = attn_norm.shape[0]
    n_mem = mem.shape[1]
    mla_heads = mla_wo.shape[1] // MLA_V

    tab_da = _rope_table(positions, DA_ROT)
    tab_mla = _rope_table(positions, MLA_ROPE)

    wkv_all = jnp.concatenate([ca_wkv[i] for i in range(depth)], axis=1).astype(BF16)
    kv_all = _norm_matmul(mem.reshape(batch * n_mem, d), mem_norm, wkv_all, BF16)
    kv_all = kv_all.reshape(batch, n_mem, wkv_all.shape[1])

    da_perm = _rope_lane_perm(DA_ROT)
    half = MLA_ROPE // 2

    def spread_rope(w):
        gap = jnp.zeros(w.shape[:-1] + (HALF_LANES - half,), w.dtype)
        return jnp.concatenate([w[..., :half], gap, w[..., half:], gap], axis=-1)

    xs = x.reshape(batch * seq, d)
    for i in range(depth):
        j = i // 2
        if i % 2 == 0:
            lambda_init = 0.8 - 0.6 * math.exp(-0.3 * i)
            w = da_wqkv[j].reshape(d, 3, d // DA_HEAD_DIM, DA_HEAD_DIM)
            w = jnp.concatenate([w[:, :2][..., da_perm], w[:, 2:]], axis=1)
            qkv = _da_qkv(xs, attn_norm[i], w.reshape(d, 3 * d).astype(BF16), tab_da)
            o = _da_attention(qkv, da_lambda[j], da_subln[j], lambda_init, batch, seq)
            wao = da_wo[j]
        else:
            q_lora = mla_q_norm.shape[1]
            wd = mla_wdown[j]
            n_lat = wd.shape[1] - MLA_ROPE
            wd_pad = jnp.concatenate([wd[:, :n_lat], spread_rope(wd[:, n_lat:])], axis=1).astype(BF16)
            wuq = mla_wuq[j].reshape(q_lora, mla_heads, MLA_NOPE + MLA_ROPE)
            wuq_pad = jnp.concatenate([wuq[..., :MLA_NOPE], spread_rope(wuq[..., MLA_NOPE:])], axis=-1)
            wuq_pad = wuq_pad.reshape(q_lora, mla_heads * MLA_QK_PAD).astype(BF16)
            q, kv, kr = _mla_proj(xs, attn_norm[i], wd_pad, mla_q_norm[j], mla_kv_norm[j], wuq_pad,
                                  mla_wukv[j].astype(BF16), tab_mla, mla_heads)
            o = _mla_attention(q, kv, kr, batch, seq, mla_heads)
            wao = mla_wo[j]
        xs = _post_attn(o, wao.astype(BF16), xs, cross_norm[i], ca_wq[i].astype(BF16), kv_all, i,
                        ca_wo[i].astype(BF16), seq)
        xs = _mlp(xs, mlp_norm[i], mlp_wup[i].astype(BF16), mlp_wdown[i].astype(BF16), final_norm,
                  final_norm=(i == depth - 1))
    return xs.reshape(batch, seq, d)
```

```python
import functools
import math

import jax
import jax.numpy as jnp
from jax import lax
from jax.experimental import pallas as pl
from jax.experimental.pallas import tpu as pltpu

F32 = jnp.float32
BF16 = jnp.bfloat16

EPS = 1e-6
ROPE_THETA = 500000.0
LANES = 128
LOG2E = 1.4426950408889634
NEG_BIG = -1e30

DA_HEAD_DIM = 128
DA_ROT = DA_HEAD_DIM // 4
MLA_NOPE = 128
MLA_ROPE = 64
MLA_V = 128
MLA_QK_PAD = 256
CA_HEADS = 4
CA_HEAD_DIM = 128

VMEM_LIMIT = 56 * 1024 * 1024
ATTN_VMEM_LIMIT = 60 * 1024 * 1024


def _cparams(*sem, vmem=VMEM_LIMIT):
    return pltpu.CompilerParams(dimension_semantics=sem, vmem_limit_bytes=vmem)


def _pick(n, target):
    t = min(n, target)
    while n % t or (t % 8 and t != n):
        t -= 1
    return t


def _rms(xf, g):
    ms = jnp.mean(xf * xf, axis=-1, keepdims=True)
    return xf * lax.rsqrt(ms + EPS) * g


HALF_LANES = LANES // 2


def _rope_chunk(xc, c, s):
    return xc * c + pltpu.roll(xc, HALF_LANES, 1) * s


def _rope_lane_perm(rot):
    half = rot // 2
    perm = list(range(LANES))
    perm[half:rot], perm[HALF_LANES:HALF_LANES + half] = perm[HALF_LANES:HALF_LANES + half], perm[half:rot]
    return jnp.array(perm, jnp.int32)


def _rope_table(positions, rot):
    half = rot // 2
    inv_freq = ROPE_THETA ** (-jnp.arange(0, rot, 2, dtype=F32) / rot)
    ang = positions.astype(F32).reshape(-1)[:, None] * inv_freq
    cos, sin = jnp.cos(ang), jnp.sin(ang)
    n = ang.shape[0]
    gap = HALF_LANES - half
    c = jnp.concatenate([cos, jnp.ones((n, gap), F32), cos, jnp.ones((n, gap), F32)], axis=1)
    s = jnp.concatenate([-sin, jnp.zeros((n, gap), F32), sin, jnp.zeros((n, gap), F32)], axis=1)
    return jnp.concatenate([c, s], axis=1)


def _norm_matmul_kernel(x_ref, g_ref, w_ref, o_ref, h_ref):
    @pl.when(pl.program_id(1) == 0)
    def _():
        h_ref[...] = _rms(x_ref[...], g_ref[...]).astype(BF16)

    o_ref[...] = jnp.dot(h_ref[...], w_ref[...], preferred_element_type=F32).astype(o_ref.dtype)


def _norm_matmul(x, g, w, out_dtype, tm_target=512, tn_target=1024):
    m, d = x.shape
    n = w.shape[1]
    tm, tn = _pick(m, tm_target), _pick(n, tn_target)
    return pl.pallas_call(
        _norm_matmul_kernel,
        grid=(m // tm, n // tn),
        in_specs=[pl.BlockSpec((tm, d), lambda i, j: (i, 0)),
                  pl.BlockSpec((1, d), lambda i, j: (0, 0)),
                  pl.BlockSpec((d, tn), lambda i, j: (0, j))],
        out_specs=pl.BlockSpec((tm, tn), lambda i, j: (i, j)),
        out_shape=jax.ShapeDtypeStruct((m, n), out_dtype),
        scratch_shapes=[pltpu.VMEM((tm, d), BF16)],
        compiler_params=_cparams("parallel", "arbitrary"),
        name="norm_matmul",
    )(x, g.reshape(1, d), w)


def _da_qkv_kernel(x_ref, g_ref, w_ref, tab_ref, o_ref, h_ref, *, n_head_blocks, q_scale):
    j = pl.program_id(1)

    @pl.when(j == 0)
    def _():
        h_ref[...] = _rms(x_ref[...], g_ref[...]).astype(BF16)

    acc = jnp.dot(h_ref[...], w_ref[...], preferred_element_type=F32)
    rotary = j < 2 * n_head_blocks
    scale = jnp.where(j < n_head_blocks, q_scale, 1.0).astype(F32)
    c = jnp.where(rotary, tab_ref[:, 0:LANES], 1.0) * scale
    s = jnp.where(rotary, tab_ref[:, LANES:2 * LANES], 0.0) * scale
    for ch in range(acc.shape[1] // LANES):
        cols = slice(ch * LANES, (ch + 1) * LANES)
        o_ref[:, cols] = _rope_chunk(acc[:, cols], c, s).astype(o_ref.dtype)


def _da_qkv(x, g, w, tab, tm_target=1024, tn_target=1024):
    m, d = x.shape
    n = w.shape[1]
    tm, tn = _pick(m, tm_target), _pick(d, tn_target)
    q_scale = LOG2E / math.sqrt(DA_HEAD_DIM)
    kern = functools.partial(_da_qkv_kernel, n_head_blocks=d // tn, q_scale=q_scale)
    return pl.pallas_call(
        kern,
        grid=(m // tm, n // tn),
        in_specs=[pl.BlockSpec((tm, d), lambda i, j: (i, 0)),
                  pl.BlockSpec((1, d), lambda i, j: (0, 0)),
                  pl.BlockSpec((d, tn), lambda i, j: (0, j)),
                  pl.BlockSpec((tm, 2 * LANES), lambda i, j: (i, 0))],
        out_specs=pl.BlockSpec((tm, tn), lambda i, j: (i, j)),
        out_shape=jax.ShapeDtypeStruct((m, n), BF16),
        scratch_shapes=[pltpu.VMEM((tm, d), BF16)],
        compiler_params=_cparams("parallel", "arbitrary"),
        name="da_qkv",
    )(x, g.reshape(1, d), w, tab)


N_BUF = 2


def _causal_sweep(qi, tk, score, softmax, accumulate):
    low = slice(tk, 2 * tk)
    score(2 * qi, 0)
    softmax(0, 0)
    score(2 * qi + 1, 1, rows=low)
    accumulate(2 * qi, 0)
    softmax(1, 0, rows=low)
    accumulate(2 * qi + 1, 1, rows=low)

    @pl.when(qi > 0)
    def _():
        score(0, 0)
        softmax(0, None)
        score(1, 1)

        def pair(u, carry):
            t = 2 * u
            accumulate(t, 0)
            softmax(1, None)
            score(t + 2, 0)
            accumulate(t + 1, 1)
            softmax(0, None)
            score(t + 3, 1)
            return carry

        lax.fori_loop(0, qi - 1, pair, 0)
        t = 2 * qi - 2
        accumulate(t, 0)
        softmax(1, None)
        accumulate(t + 1, 1)


def _softmax_update(s, diag, m_prev, l_prev):
    tq, tk = s.shape
    if diag is not None:
        row = lax.broadcasted_iota(jnp.int32, (tq, tk), 0)
        col = lax.broadcasted_iota(jnp.int32, (tq, tk), 1)
        s = jnp.where(col + diag <= row, s, NEG_BIG)
    m_next = jnp.maximum(m_prev, jnp.max(s, axis=1, keepdims=True))
    alpha = jnp.exp2(m_prev - m_next)
    p = jnp.exp2(s - jnp.tile(m_next, (1, tk // LANES)))
    part = p[:, 0:LANES]
    for ch in range(1, tk // LANES):
        part = part + p[:, ch * LANES:(ch + 1) * LANES]
    return m_next, alpha * l_prev + part, alpha, p.astype(BF16)


def _attn_stages(n_streams, tk, q_of, k_of, v_of, s_ref, p_ref, a_ref, m_ref, l_ref, acc_ref):
    nt = (((1,), (1,)), ((), ()))
    every = slice(None)

    def score(blk, par, rows=every):
        keys = pl.ds(pl.multiple_of(blk * tk, tk), tk)
        for c in range(n_streams):
            s_ref[par, c, rows] = lax.dot_general(q_of(c, rows), k_of(c, keys), nt, preferred_element_type=F32)

    def softmax(par, diag, rows=every):
        for c in range(n_streams):
            m_next, l_next, alpha, p = _softmax_update(s_ref[par, c, rows], diag, m_ref[c, rows], l_ref[c, rows])
            m_ref[c, rows] = m_next
            l_ref[c, rows] = l_next
            a_ref[par, c, rows] = alpha
            p_ref[par, c, rows] = p

    def accumulate(blk, par, rows=every):
        v = v_of(pl.ds(pl.multiple_of(blk * tk, tk), tk))
        rep = acc_ref.shape[-1] // LANES
        for c in range(n_streams):
            pv = jnp.dot(p_ref[par, c, rows], v, preferred_element_type=F32)
            acc_ref[c, rows] = acc_ref[c, rows] * jnp.tile(a_ref[par, c, rows], (1, rep)) + pv

    return score, softmax, accumulate


def _attn_scratch(n_streams, tq, tk, vw):
    return [pltpu.VMEM((N_BUF, n_streams, tq, tk), F32),
            pltpu.VMEM((N_BUF, n_streams, tq, tk), BF16),
            pltpu.VMEM((N_BUF, n_streams, tq, LANES), F32),
            pltpu.VMEM((n_streams, tq, LANES), F32),
            pltpu.VMEM((n_streams, tq, LANES), F32),
            pltpu.VMEM((n_streams, tq, vw), F32)]


def _da_attn_kernel(q_ref, k_ref, v_ref, lam_ref, sub_ref, o_ref, s_ref, p_ref, a_ref, m_ref, l_ref, acc_ref,
                    *, tk, lambda_init):
    qi = pl.program_id(2)
    hd = DA_HEAD_DIM
    m_ref[...] = jnp.full(m_ref.shape, NEG_BIG, F32)
    l_ref[...] = jnp.zeros(l_ref.shape, F32)
    acc_ref[...] = jnp.zeros(acc_ref.shape, F32)

    stages = _attn_stages(
        2, tk,
        lambda c, rows: q_ref[rows, c * hd:(c + 1) * hd],
        lambda c, keys: k_ref[keys, c * hd:(c + 1) * hd],
        lambda rows: v_ref[rows, :],
        s_ref, p_ref, a_ref, m_ref, l_ref, acc_ref)
    _causal_sweep(qi, tk, *stages)

    lv = lam_ref[...]
    lam = (jnp.exp(jnp.sum(lv[0:1] * lv[1:2], axis=1, keepdims=True))
           - jnp.exp(jnp.sum(lv[2:3] * lv[3:4], axis=1, keepdims=True)) + lambda_init)
    l0 = jnp.sum(l_ref[0], axis=1, keepdims=True)
    l1 = jnp.sum(l_ref[1], axis=1, keepdims=True)
    o = acc_ref[0] / l0 - lam * (acc_ref[1] / l1)
    o_ref[...] = (_rms(o, sub_ref[...]) * (1.0 - lambda_init)).astype(o_ref.dtype)


def _attn_blocks(seq, tk_target=512):
    tk = _pick(seq // 2, tk_target)
    assert seq % (2 * tk) == 0 and tk % LANES == 0
    return 2 * tk, tk


def _da_attention(qkv, lam_vecs, subln, lambda_init, batch, seq):
    d3 = qkv.shape[1]
    d = d3 // 3
    hw = 2 * DA_HEAD_DIM
    heads = d // hw
    tq, tk = _attn_blocks(seq)
    qkv3 = qkv.reshape(batch, seq, d3)
    kern = functools.partial(_da_attn_kernel, tk=tk, lambda_init=lambda_init)
    resident = pl.Buffered(1)
    out = pl.pallas_call(
        kern,
        grid=(batch, heads, seq // tq),
        in_specs=[pl.BlockSpec((None, tq, hw), lambda b, h, i: (b, i, h)),
                  pl.BlockSpec((None, seq, hw), lambda b, h, i: (b, 0, heads + h), pipeline_mode=resident),
                  pl.BlockSpec((None, seq, hw), lambda b, h, i: (b, 0, 2 * heads + h), pipeline_mode=resident),
                  pl.BlockSpec((4, DA_HEAD_DIM), lambda b, h, i: (0, 0)),
                  pl.BlockSpec((1, hw), lambda b, h, i: (0, 0))],
        out_specs=pl.BlockSpec((None, tq, hw), lambda b, h, i: (b, i, h)),
        out_shape=jax.ShapeDtypeStruct((batch, seq, d), BF16),
        scratch_shapes=_attn_scratch(2, tq, tk, hw),
        compiler_params=_cparams("parallel", "parallel", "arbitrary", vmem=ATTN_VMEM_LIMIT),
        name="da_attn",
    )(qkv3, qkv3, qkv3, lam_vecs, subln.reshape(1, hw))
    return out.reshape(batch * seq, d)


def _mla_proj_kernel(x_ref, g_ref, wd_ref, qn_ref, kvn_ref, wuq_ref, wukv_ref, tab_ref,
                     q_ref, kv_ref, kr_ref, *, q_lora, kv_lora, heads, q_scale):
    h = _rms(x_ref[...], g_ref[...]).astype(BF16)
    d = jnp.dot(h, wd_ref[...], preferred_element_type=F32)
    rc = tab_ref[:, 0:LANES]
    rs = tab_ref[:, LANES:2 * LANES]
    c_q = _rms(d[:, :q_lora], qn_ref[...]).astype(BF16)
    c_kv = _rms(d[:, q_lora:q_lora + kv_lora], kvn_ref[...]).astype(BF16)
    kr = _rope_chunk(d[:, q_lora + kv_lora:q_lora + kv_lora + LANES], rc, rs)
    kr_ref[...] = kr.astype(kr_ref.dtype)
    w = MLA_QK_PAD
    qc, qs = rc * q_scale, rs * q_scale
    for hh in range(heads):
        qh = jnp.dot(c_q, wuq_ref[:, hh * w:(hh + 1) * w], preferred_element_type=F32)
        q_ref[:, hh * w:hh * w + LANES] = (qh[:, :LANES] * q_scale).astype(q_ref.dtype)
        q_ref[:, hh * w + LANES:(hh + 1) * w] = _rope_chunk(qh[:, LANES:], qc, qs).astype(q_ref.dtype)
        kv_ref[:, hh * w:(hh + 1) * w] = jnp.dot(
            c_kv, wukv_ref[:, hh * w:(hh + 1) * w], preferred_element_type=F32).astype(kv_ref.dtype)


def _mla_proj(x, g, wd_pad, qn, kvn, wuq_pad, wukv, tab, heads, tm_target=256):
    m, d = x.shape
    tm = _pick(m, tm_target)
    q_lora, kv_lora = qn.shape[0], kvn.shape[0]
    q_scale = LOG2E / math.sqrt(MLA_NOPE + MLA_ROPE)
    kern = functools.partial(_mla_proj_kernel, q_lora=q_lora, kv_lora=kv_lora, heads=heads, q_scale=q_scale)
    const = lambda i: (0, 0)
    row = lambda i: (i, 0)
    nq, nkv = wuq_pad.shape[1], wukv.shape[1]
    return pl.pallas_call(
        kern,
        grid=(m // tm,),
        in_specs=[pl.BlockSpec((tm, d), row),
                  pl.BlockSpec((1, d), const),
                  pl.BlockSpec(wd_pad.shape, const),
                  pl.BlockSpec((1, q_lora), const),
                  pl.BlockSpec((1, kv_lora), const),
                  pl.BlockSpec(wuq_pad.shape, const),
                  pl.BlockSpec(wukv.shape, const),
                  pl.BlockSpec((tm, 2 * LANES), row)],
        out_specs=[pl.BlockSpec((tm, nq), row),
                   pl.BlockSpec((tm, nkv), row),
                   pl.BlockSpec((tm, LANES), row)],
        out_shape=[jax.ShapeDtypeStruct((m, nq), BF16),
                   jax.ShapeDtypeStruct((m, nkv), BF16),
                   jax.ShapeDtypeStruct((m, LANES), BF16)],
        compiler_params=_cparams("parallel"),
        name="mla_proj",
    )(x, g.reshape(1, d), wd_pad, qn.reshape(1, q_lora), kvn.reshape(1, kv_lora), wuq_pad, wukv, tab)


def _mla_attn_kernel(q_ref, kn_ref, kr_ref, v_ref, o_ref, s_ref, p_ref, a_ref, m_ref, l_ref, acc_ref, *, tk):
    qi = pl.program_id(2)
    m_ref[...] = jnp.full(m_ref.shape, NEG_BIG, F32)
    l_ref[...] = jnp.zeros(l_ref.shape, F32)
    acc_ref[...] = jnp.zeros(acc_ref.shape, F32)

    stages = _attn_stages(
        1, tk,
        lambda c, rows: q_ref[rows, :],
        lambda c, keys: jnp.concatenate([kn_ref[keys, :], kr_ref[keys, :]], axis=1),
        lambda rows: v_ref[rows, :],
        s_ref, p_ref, a_ref, m_ref, l_ref, acc_ref)
    _causal_sweep(qi, tk, *stages)
    o_ref[...] = (acc_ref[0] / jnp.sum(l_ref[0], axis=1, keepdims=True)).astype(o_ref.dtype)


def _mla_attention(q, kv, kr, batch, seq, heads):
    tq, tk = _attn_blocks(seq)
    w = MLA_QK_PAD
    q3 = q.reshape(batch, seq, heads * w)
    kv3 = kv.reshape(batch, seq, heads * w)
    kr3 = kr.reshape(batch, seq, LANES)
    kern = functools.partial(_mla_attn_kernel, tk=tk)
    out = pl.pallas_call(
        kern,
        grid=(batch, heads, seq // tq),
        in_specs=[pl.BlockSpec((None, tq, w), lambda b, h, i: (b, i, h)),
                  pl.BlockSpec((None, seq, LANES), lambda b, h, i: (b, 0, 2 * h)),
                  pl.BlockSpec((None, seq, LANES), lambda b, h, i: (b, 0, 0)),
                  pl.BlockSpec((None, seq, LANES), lambda b, h, i: (b, 0, 2 * h + 1))],
        out_specs=pl.BlockSpec((None, tq, MLA_V), lambda b, h, i: (b, i, h)),
        out_shape=jax.ShapeDtypeStruct((batch, seq, heads * MLA_V), BF16),
        scratch_shapes=_attn_scratch(1, tq, tk, MLA_V),
        compiler_params=_cparams("parallel", "parallel", "arbitrary", vmem=ATTN_VMEM_LIMIT),
        name="mla_attn",
    )(q3, kv3, kr3, kv3)
    return out.reshape(batch * seq, heads * MLA_V)


def _post_attn_kernel(o_ref, wao_ref, x_ref, g_ref, wq_ref, kv_ref, wo_ref, out_ref, *, q_scale):
    x = x_ref[...] + jnp.dot(o_ref[...], wao_ref[...], preferred_element_type=F32)
    h = _rms(x, g_ref[...]).astype(BF16)
    q = (jnp.dot(h, wq_ref[...], preferred_element_type=F32) * q_scale).astype(BF16)
    hd = CA_HEAD_DIM
    kw = CA_HEADS * hd
    outs = []
    for hh in range(CA_HEADS):
        k = kv_ref[:, hh * hd:(hh + 1) * hd]
        v = kv_ref[:, kw + hh * hd:kw + (hh + 1) * hd]
        s = lax.dot_general(q[:, hh * hd:(hh + 1) * hd], k, (((1,), (1,)), ((), ())),
                            preferred_element_type=F32)
        p = jnp.exp2(s - jnp.max(s, axis=1, keepdims=True))
        l = jnp.sum(p, axis=1, keepdims=True)
        outs.append((jnp.dot(p.astype(BF16), v, preferred_element_type=F32) / l).astype(BF16))
    o = jnp.concatenate(outs, axis=1)
    out_ref[...] = x + jnp.dot(o, wo_ref[...], preferred_element_type=F32)


def _post_attn(o, wao, x, g, wq, kv_all, layer, wo, seq, tm_target=512):
    m, d = x.shape
    tm = _pick(seq, tm_target)
    blocks_per_batch = seq // tm
    n_mem = kv_all.shape[1]
    kvw = 2 * CA_HEADS * CA_HEAD_DIM
    kern = functools.partial(_post_attn_kernel, q_scale=LOG2E / math.sqrt(CA_HEAD_DIM))
    const = lambda i: (0, 0)
    once = pl.Buffered(1)
    return pl.pallas_call(
        kern,
        grid=(m // tm,),
        in_specs=[pl.BlockSpec((tm, o.shape[1]), lambda i: (i, 0)),
                  pl.BlockSpec(wao.shape, const, pipeline_mode=once),
                  pl.BlockSpec((tm, d), lambda i: (i, 0)),
                  pl.BlockSpec((1, d), const),
                  pl.BlockSpec(wq.shape, const, pipeline_mode=once),
                  pl.BlockSpec((None, n_mem, kvw), lambda i: (i // blocks_per_batch, 0, layer)),
                  pl.BlockSpec(wo.shape, const, pipeline_mode=once)],
        out_specs=pl.BlockSpec((tm, d), lambda i: (i, 0)),
        out_shape=jax.ShapeDtypeStruct((m, d), F32),
        compiler_params=_cparams("parallel"),
        name="post_attn",
    )(o, wao, x, g.reshape(1, d), wq, kv_all, wo)


def _mlp_kernel(x_ref, g_ref, wup_ref, wdown_ref, fg_ref, out_ref, h_ref, *, final_norm):
    f = pl.program_id(1)

    @pl.when(f == 0)
    def _():
        x = x_ref[...]
        h_ref[...] = _rms(x, g_ref[...]).astype(BF16)
        out_ref[...] = x

    u = jnp.maximum(jnp.dot(h_ref[...], wup_ref[...], preferred_element_type=F32), 0.0)
    out_ref[...] += jnp.dot((u * u).astype(BF16), wdown_ref[...], preferred_element_type=F32)

    if final_norm:
        @pl.when(f == pl.num_programs(1) - 1)
        def _():
            out_ref[...] = _rms(out_ref[...], fg_ref[...])


def _mlp(x, g, wup, wdown, final_g, final_norm, tm_target=1024, tf_target=512):
    m, d = x.shape
    ff = wup.shape[1]
    tm, tf = _pick(m, tm_target), _pick(ff, tf_target)
    kern = functools.partial(_mlp_kernel, final_norm=final_norm)
    return pl.pallas_call(
        kern,
        grid=(m // tm, ff // tf),
        in_specs=[pl.BlockSpec((tm, d), lambda i, f: (i, 0)),
                  pl.BlockSpec((1, d), lambda i, f: (0, 0)),
                  pl.BlockSpec((d, tf), lambda i, f: (0, f)),
                  pl.BlockSpec((tf, d), lambda i, f: (f, 0)),
                  pl.BlockSpec((1, d), lambda i, f: (0, 0))],
        out_specs=pl.BlockSpec((tm, d), lambda i, f: (i, 0)),
        out_shape=jax.ShapeDtypeStruct((m, d), F32),
        scratch_shapes=[pltpu.VMEM((tm, d), BF16)],
        compiler_params=_cparams("parallel", "arbitrary"),
        name="mlp",
    )(x, g.reshape(1, d), wup, wdown, final_g.reshape(1, d))


def kernel(x, mem, positions, attn_norm, cross_norm, mlp_norm, mem_norm, final_norm, da_wqkv, da_lambda, da_subln, da_wo, mla_wdown, mla_q_norm, mla_kv_norm, mla_wuq, mla_wukv, mla_wo, ca_wq, ca_wkv, ca_wo, mlp_wup, mlp_wdown):
    batch, seq, d = x.shape
    depth = attn_norm.shape[0]
    n_mem = mem.shape[1]
    mla_heads = mla_wo.shape[1] // MLA_V

    tab_da = _rope_table(positions, DA_ROT)
    tab_mla = _rope_table(positions, MLA_ROPE)

    wkv_all = jnp.concatenate([ca_wkv[i] for i in range(depth)], axis=1).astype(BF16)
    kv_all = _norm_matmul(mem.reshape(batch * n_mem, d), mem_norm, wkv_all, BF16)
    kv_all = kv_all.reshape(batch, n_mem, wkv_all.shape[1])

    da_perm = _rope_lane_perm(DA_ROT)
    half = MLA_ROPE // 2

    def spread_rope(w):
        gap = jnp.zeros(w.shape[:-1] + (HALF_LANES - half,), w.dtype)
        return jnp.concatenate([w[..., :half], gap, w[..., half:], gap], axis=-1)

    xs = x.reshape(batch * seq, d)
    for i in range(depth):
        j = i // 2
        if i % 2 == 0:
            lambda_init = 0.8 - 0.6 * math.exp(-0.3 * i)
            w = da_wqkv[j].reshape(d, 3, d // DA_HEAD_DIM, DA_HEAD_DIM)
            w = jnp.concatenate([w[:, :2][..., da_perm], w[:, 2:]], axis=1)
            qkv = _da_qkv(xs, attn_norm[i], w.reshape(d, 3 * d).astype(BF16), tab_da)
            o = _da_attention(qkv, da_lambda[j], da_subln[j], lambda_init, batch, seq)
            wao = da_wo[j]
        else:
            q_lora = mla_q_norm.shape[1]
            wd = mla_wdown[j]
            n_lat = wd.shape[1] - MLA_ROPE
            wd_pad = jnp.concatenate([wd[:, :n_lat], spread_rope(wd[:, n_lat:])], axis=1).astype(BF16)
            wuq = mla_wuq[j].reshape(q_lora, mla_heads, MLA_NOPE + MLA_ROPE)
            wuq_pad = jnp.concatenate([wuq[..., :MLA_NOPE], spread_rope(wuq[..., MLA_NOPE:])], axis=-1)
            wuq_pad = wuq_pad.reshape(q_lora, mla_heads * MLA_QK_PAD).astype(BF16)
            q, kv, kr = _mla_proj(xs, attn_norm[i], wd_pad, mla_q_norm[j], mla_kv_norm[j], wuq_pad,
                                  mla_wukv[j].astype(BF16), tab_mla, mla_heads)
            o = _mla_attention(q, kv, kr, batch, seq, mla_heads)
            wao = mla_wo[j]
        xs = _post_attn(o, wao.astype(BF16), xs, cross_norm[i], ca_wq[i].astype(BF16), kv_all, i,
                        ca_wo[i].astype(BF16), seq)
        xs = _mlp(xs, mlp_norm[i], mlp_wup[i].astype(BF16), mlp_wdown[i].astype(BF16), final_norm,
                  final_norm=(i == depth - 1))
    return xs.reshape(batch, seq, d)
```

```python
import functools
import math

import jax
import jax.numpy as jnp
from jax import lax
from jax.experimental import pallas as pl
from jax.experimental.pallas import tpu as pltpu

F32 = jnp.float32
BF16 = jnp.bfloat16

EPS = 1e-6
ROPE_THETA = 500000.0
LANES = 128
LOG2E = 1.4426950408889634
NEG_BIG = -1e30

DA_HEAD_DIM = 128
DA_ROT = DA_HEAD_DIM // 4
MLA_NOPE = 128
MLA_ROPE = 64
MLA_V = 128
MLA_QK_PAD = 256
CA_HEADS = 4
CA_HEAD_DIM = 128

VMEM_LIMIT = 56 * 1024 * 1024
ATTN_VMEM_LIMIT = 60 * 1024 * 1024


def _cparams(*sem, vmem=VMEM_LIMIT):
    return pltpu.CompilerParams(dimension_semantics=sem, vmem_limit_bytes=vmem)


def _pick(n, target):
    t = min(n, target)
    while n % t or (t % 8 and t != n):
        t -= 1
    return t


def _rms(xf, g):
    ms = jnp.mean(xf * xf, axis=-1, keepdims=True)
    return xf * lax.rsqrt(ms + EPS) * g


HALF_LANES = LANES // 2


def _rope_chunk(xc, c, s):
    return xc * c + pltpu.roll(xc, HALF_LANES, 1) * s


def _rope_lane_perm(w, rot):
    half = rot // 2
    return jnp.concatenate([w[..., :half], w[..., HALF_LANES:HALF_LANES + half], w[..., rot:HALF_LANES],
                            w[..., half:rot], w[..., HALF_LANES + half:]], axis=-1)


def _rope_table(positions, rot):
    half = rot // 2
    inv_freq = ROPE_THETA ** (-jnp.arange(0, rot, 2, dtype=F32) / rot)
    ang = positions.astype(F32).reshape(-1)[:, None] * inv_freq
    cos, sin = jnp.cos(ang), jnp.sin(ang)
    n = ang.shape[0]
    gap = HALF_LANES - half
    c = jnp.concatenate([cos, jnp.ones((n, gap), F32), cos, jnp.ones((n, gap), F32)], axis=1)
    s = jnp.concatenate([-sin, jnp.zeros((n, gap), F32), sin, jnp.zeros((n, gap), F32)], axis=1)
    return jnp.concatenate([c, s], axis=1)


def _norm_matmul_kernel(x_ref, g_ref, w_ref, o_ref, h_ref):
    @pl.when(pl.program_id(1) == 0)
    def _():
        h_ref[...] = _rms(x_ref[...], g_ref[...]).astype(BF16)

    o_ref[...] = jnp.dot(h_ref[...], w_ref[...], preferred_element_type=F32).astype(o_ref.dtype)


def _norm_matmul(x, g, w, out_dtype, tm_target=512, tn_target=1024):
    m, d = x.shape
    n = w.shape[1]
    tm, tn = _pick(m, tm_target), _pick(n, tn_target)
    return pl.pallas_call(
        _norm_matmul_kernel,
        grid=(m // tm, n // tn),
        in_specs=[pl.BlockSpec((tm, d), lambda i, j: (i, 0)),
                  pl.BlockSpec((1, d), lambda i, j: (0, 0)),
                  pl.BlockSpec((d, tn), lambda i, j: (0, j))],
        out_specs=pl.BlockSpec((tm, tn), lambda i, j: (i, j)),
        out_shape=jax.ShapeDtypeStruct((m, n), out_dtype),
        scratch_shapes=[pltpu.VMEM((tm, d), BF16)],
        compiler_params=_cparams("parallel", "arbitrary"),
        name="norm_matmul",
    )(x, g.reshape(1, d), w)


def _da_qkv_kernel(x_ref, g_ref, w_ref, tab_ref, o_ref, h_ref, *, n_head_blocks, q_scale):
    j = pl.program_id(1)

    @pl.when(j == 0)
    def _():
        h_ref[...] = _rms(x_ref[...], g_ref[...]).astype(BF16)

    acc = jnp.dot(h_ref[...], w_ref[...], preferred_element_type=F32)
    rotary = j < 2 * n_head_blocks
    scale = jnp.where(j < n_head_blocks, q_scale, 1.0).astype(F32)
    c = jnp.where(rotary, tab_ref[:, 0:LANES], 1.0) * scale
    s = jnp.where(rotary, tab_ref[:, LANES:2 * LANES], 0.0) * scale
    for ch in range(acc.shape[1] // LANES):
        cols = slice(ch * LANES, (ch + 1) * LANES)
        o_ref[:, cols] = _rope_chunk(acc[:, cols], c, s).astype(o_ref.dtype)


def _da_qkv(x, g, w, tab, tm_target=1024, tn_target=1024):
    m, d = x.shape
    n = w.shape[1]
    tm, tn = _pick(m, tm_target), _pick(d, tn_target)
    q_scale = LOG2E / math.sqrt(DA_HEAD_DIM)
    kern = functools.partial(_da_qkv_kernel, n_head_blocks=d // tn, q_scale=q_scale)
    return pl.pallas_call(
        kern,
        grid=(m // tm, n // tn),
        in_specs=[pl.BlockSpec((tm, d), lambda i, j: (i, 0)),
                  pl.BlockSpec((1, d), lambda i, j: (0, 0)),
                  pl.BlockSpec((d, tn), lambda i, j: (0, j)),
                  pl.BlockSpec((tm, 2 * LANES), lambda i, j: (i, 0))],
        out_specs=pl.BlockSpec((tm, tn), lambda i, j: (i, j)),
        out_shape=jax.ShapeDtypeStruct((m, n), BF16),
        scratch_shapes=[pltpu.VMEM((tm, d), BF16)],
        compiler_params=_cparams("parallel", "arbitrary"),
        name="da_qkv",
    )(x, g.reshape(1, d), w, tab)


N_BUF = 2


def _causal_sweep(qi, tk, ratio, score, softmax, accumulate):
    first = ratio * qi
    below = [slice(j * tk, ratio * tk) for j in range(ratio)]
    score(first, 0)
    softmax(0, 0)
    for j in range(1, ratio):
        score(first + j, j % N_BUF, rows=below[j])
        accumulate(first + j - 1, (j - 1) % N_BUF, rows=below[j - 1])
        softmax(j % N_BUF, 0, rows=below[j])
    accumulate(first + ratio - 1, (ratio - 1) % N_BUF, rows=below[ratio - 1])

    @pl.when(qi > 0)
    def _():
        score(0, 0)
        softmax(0, None)
        score(1, 1)

        def pair(u, carry):
            t = 2 * u
            accumulate(t, 0)
            softmax(1, None)
            score(t + 2, 0)
            accumulate(t + 1, 1)
            softmax(0, None)
            score(t + 3, 1)
            return carry

        lax.fori_loop(0, first // 2 - 1, pair, 0)
        t = first - 2
        accumulate(t, 0)
        softmax(1, None)
        accumulate(t + 1, 1)


def _softmax_update(s, diag, m_prev, l_prev):
    tq, tk = s.shape
    if diag is not None:
        row = lax.broadcasted_iota(jnp.int32, (tq, tk), 0)
        col = lax.broadcasted_iota(jnp.int32, (tq, tk), 1)
        s = jnp.where(col + diag <= row, s, NEG_BIG)
    m_next = jnp.maximum(m_prev, jnp.max(s, axis=1, keepdims=True))
    alpha = jnp.exp2(m_prev - m_next)
    p = jnp.exp2(s - jnp.tile(m_next, (1, tk // LANES)))
    part = p[:, 0:LANES]
    for ch in range(1, tk // LANES):
        part = part + p[:, ch * LANES:(ch + 1) * LANES]
    return m_next, alpha * l_prev + part, alpha, p.astype(BF16)


def _attn_stages(n_streams, tk, q_of, k_of, v_of, s_ref, p_ref, a_ref, m_ref, l_ref, acc_ref):
    nt = (((1,), (1,)), ((), ()))
    every = slice(None)

    def score(blk, par, rows=every):
        keys = pl.ds(pl.multiple_of(blk * tk, tk), tk)
        for c in range(n_streams):
            s_ref[par, c, rows] = lax.dot_general(q_of(c, rows), k_of(c, keys), nt, preferred_element_type=F32)

    def softmax(par, diag, rows=every):
        for c in range(n_streams):
            m_next, l_next, alpha, p = _softmax_update(s_ref[par, c, rows], diag, m_ref[c, rows], l_ref[c, rows])
            m_ref[c, rows] = m_next
            l_ref[c, rows] = l_next
            a_ref[par, c, rows] = alpha
            p_ref[par, c, rows] = p

    def accumulate(blk, par, rows=every):
        v = v_of(pl.ds(pl.multiple_of(blk * tk, tk), tk))
        rep = acc_ref.shape[-1] // LANES
        for c in range(n_streams):
            pv = jnp.dot(p_ref[par, c, rows], v, preferred_element_type=F32)
            acc_ref[c, rows] = acc_ref[c, rows] * jnp.tile(a_ref[par, c, rows], (1, rep)) + pv

    return score, softmax, accumulate


def _attn_scratch(n_streams, tq, tk, vw):
    return [pltpu.VMEM((N_BUF, n_streams, tq, tk), F32),
            pltpu.VMEM((N_BUF, n_streams, tq, tk), BF16),
            pltpu.VMEM((N_BUF, n_streams, tq, LANES), F32),
            pltpu.VMEM((n_streams, tq, LANES), F32),
            pltpu.VMEM((n_streams, tq, LANES), F32),
            pltpu.VMEM((n_streams, tq, vw), F32)]


def _da_attn_kernel(q_ref, k_ref, v_ref, lam_ref, sub_ref, o_ref, s_ref, p_ref, a_ref, m_ref, l_ref, acc_ref,
                    *, tk, ratio, lambda_init):
    qi = pl.program_id(2)
    hd = DA_HEAD_DIM
    m_ref[...] = jnp.full(m_ref.shape, NEG_BIG, F32)
    l_ref[...] = jnp.zeros(l_ref.shape, F32)
    acc_ref[...] = jnp.zeros(acc_ref.shape, F32)

    stages = _attn_stages(
        2, tk,
        lambda c, rows: q_ref[rows, c * hd:(c + 1) * hd],
        lambda c, keys: k_ref[keys, c * hd:(c + 1) * hd],
        lambda rows: v_ref[rows, :],
        s_ref, p_ref, a_ref, m_ref, l_ref, acc_ref)
    _causal_sweep(qi, tk, ratio, *stages)

    lv = lam_ref[...]
    lam = (jnp.exp(jnp.sum(lv[0:1] * lv[1:2], axis=1, keepdims=True))
           - jnp.exp(jnp.sum(lv[2:3] * lv[3:4], axis=1, keepdims=True)) + lambda_init)
    l0 = jnp.sum(l_ref[0], axis=1, keepdims=True)
    l1 = jnp.sum(l_ref[1], axis=1, keepdims=True)
    o = acc_ref[0] / l0 - lam * (acc_ref[1] / l1)
    o_ref[...] = (_rms(o, sub_ref[...]) * (1.0 - lambda_init)).astype(o_ref.dtype)


DA_RATIO = 2
MLA_RATIO = 4


def _attn_blocks(seq, ratio, tk_target=512):
    tk = _pick(seq // ratio, tk_target)
    assert ratio % 2 == 0 and seq % (ratio * tk) == 0 and tk % LANES == 0
    return ratio * tk, tk


def _da_attention(qkv, lam_vecs, subln, lambda_init, batch, seq):
    d3 = qkv.shape[1]
    d = d3 // 3
    hw = 2 * DA_HEAD_DIM
    heads = d // hw
    tq, tk = _attn_blocks(seq, DA_RATIO)
    qkv3 = qkv.reshape(batch, seq, d3)
    kern = functools.partial(_da_attn_kernel, tk=tk, ratio=DA_RATIO, lambda_init=lambda_init)
    resident = pl.Buffered(1)
    out = pl.pallas_call(
        kern,
        grid=(batch, heads, seq // tq),
        in_specs=[pl.BlockSpec((None, tq, hw), lambda b, h, i: (b, i, h)),
                  pl.BlockSpec((None, seq, hw), lambda b, h, i: (b, 0, heads + h), pipeline_mode=resident),
                  pl.BlockSpec((None, seq, hw), lambda b, h, i: (b, 0, 2 * heads + h), pipeline_mode=resident),
                  pl.BlockSpec((4, DA_HEAD_DIM), lambda b, h, i: (0, 0)),
                  pl.BlockSpec((1, hw), lambda b, h, i: (0, 0))],
        out_specs=pl.BlockSpec((None, tq, hw), lambda b, h, i: (b, i, h)),
        out_shape=jax.ShapeDtypeStruct((batch, seq, d), BF16),
        scratch_shapes=_attn_scratch(2, tq, tk, hw),
        compiler_params=_cparams("parallel", "parallel", "arbitrary", vmem=ATTN_VMEM_LIMIT),
        name="da_attn",
    )(qkv3, qkv3, qkv3, lam_vecs, subln.reshape(1, hw))
    return out.reshape(batch * seq, d)


def _mla_proj_kernel(x_ref, g_ref, wd_ref, qn_ref, kvn_ref, wuq_ref, wukv_ref, tab_ref,
                     q_ref, kv_ref, kr_ref, *, q_lora, kv_lora, heads, q_scale):
    h = _rms(x_ref[...], g_ref[...]).astype(BF16)
    d = jnp.dot(h, wd_ref[...], preferred_element_type=F32)
    rc = tab_ref[:, 0:LANES]
    rs = tab_ref[:, LANES:2 * LANES]
    c_q = _rms(d[:, :q_lora], qn_ref[...]).astype(BF16)
    c_kv = _rms(d[:, q_lora:q_lora + kv_lora], kvn_ref[...]).astype(BF16)
    kr = _rope_chunk(d[:, q_lora + kv_lora:q_lora + kv_lora + LANES], rc, rs)
    kr_ref[...] = kr.astype(kr_ref.dtype)
    w = MLA_QK_PAD
    qc, qs = rc * q_scale, rs * q_scale
    for hh in range(heads):
        qh = jnp.dot(c_q, wuq_ref[:, hh * w:(hh + 1) * w], preferred_element_type=F32)
        q_ref[:, hh * w:hh * w + LANES] = (qh[:, :LANES] * q_scale).astype(q_ref.dtype)
        q_ref[:, hh * w + LANES:(hh + 1) * w] = _rope_chunk(qh[:, LANES:], qc, qs).astype(q_ref.dtype)
        kv_ref[:, hh * w:(hh + 1) * w] = jnp.dot(
            c_kv, wukv_ref[:, hh * w:(hh + 1) * w], preferred_element_type=F32).astype(kv_ref.dtype)


def _mla_proj(x, g, wd_pad, qn, kvn, wuq_pad, wukv, tab, heads, tm_target=256):
    m, d = x.shape
    tm = _pick(m, tm_target)
    q_lora, kv_lora = qn.shape[0], kvn.shape[0]
    q_scale = LOG2E / math.sqrt(MLA_NOPE + MLA_ROPE)
    kern = functools.partial(_mla_proj_kernel, q_lora=q_lora, kv_lora=kv_lora, heads=heads, q_scale=q_scale)
    const = lambda i: (0, 0)
    row = lambda i: (i, 0)
    nq, nkv = wuq_pad.shape[1], wukv.shape[1]
    return pl.pallas_call(
        kern,
        grid=(m // tm,),
        in_specs=[pl.BlockSpec((tm, d), row),
                  pl.BlockSpec((1, d), const),
                  pl.BlockSpec(wd_pad.shape, const),
                  pl.BlockSpec((1, q_lora), const),
                  pl.BlockSpec((1, kv_lora), const),
                  pl.BlockSpec(wuq_pad.shape, const),
                  pl.BlockSpec(wukv.shape, const),
                  pl.BlockSpec((tm, 2 * LANES), row)],
        out_specs=[pl.BlockSpec((tm, nq), row),
                   pl.BlockSpec((tm, nkv), row),
                   pl.BlockSpec((tm, LANES), row)],
        out_shape=[jax.ShapeDtypeStruct((m, nq), BF16),
                   jax.ShapeDtypeStruct((m, nkv), BF16),
                   jax.ShapeDtypeStruct((m, LANES), BF16)],
        compiler_params=_cparams("parallel"),
        name="mla_proj",
    )(x, g.reshape(1, d), wd_pad, qn.reshape(1, q_lora), kvn.reshape(1, kv_lora), wuq_pad, wukv, tab)


def _mla_attn_kernel(q_ref, kn_ref, kr_ref, v_ref, o_ref, s_ref, p_ref, a_ref, m_ref, l_ref, acc_ref,
                     *, tk, ratio):
    qi = pl.program_id(2)
    m_ref[...] = jnp.full(m_ref.shape, NEG_BIG, F32)
    l_ref[...] = jnp.zeros(l_ref.shape, F32)
    acc_ref[...] = jnp.zeros(acc_ref.shape, F32)

    stages = _attn_stages(
        1, tk,
        lambda c, rows: q_ref[rows, :],
        lambda c, keys: jnp.concatenate([kn_ref[keys, :], kr_ref[keys, :]], axis=1),
        lambda rows: v_ref[rows, :],
        s_ref, p_ref, a_ref, m_ref, l_ref, acc_ref)
    _causal_sweep(qi, tk, ratio, *stages)
    o_ref[...] = (acc_ref[0] / jnp.sum(l_ref[0], axis=1, keepdims=True)).astype(o_ref.dtype)


def _mla_attention(q, kv, kr, batch, seq, heads):
    tq, tk = _attn_blocks(seq, MLA_RATIO)
    w = MLA_QK_PAD
    q3 = q.reshape(batch, seq, heads * w)
    kv3 = kv.reshape(batch, seq, heads * w)
    kr3 = kr.reshape(batch, seq, LANES)
    kern = functools.partial(_mla_attn_kernel, tk=tk, ratio=MLA_RATIO)
    out = pl.pallas_call(
        kern,
        grid=(batch, heads, seq // tq),
        in_specs=[pl.BlockSpec((None, tq, w), lambda b, h, i: (b, i, h)),
                  pl.BlockSpec((None, seq, LANES), lambda b, h, i: (b, 0, 2 * h)),
                  pl.BlockSpec((None, seq, LANES), lambda b, h, i: (b, 0, 0)),
                  pl.BlockSpec((None, seq, LANES), lambda b, h, i: (b, 0, 2 * h + 1))],
        out_specs=pl.BlockSpec((None, tq, MLA_V), lambda b, h, i: (b, i, h)),
        out_shape=jax.ShapeDtypeStruct((batch, seq, heads * MLA_V), BF16),
        scratch_shapes=_attn_scratch(1, tq, tk, MLA_V),
        compiler_params=_cparams("parallel", "parallel", "arbitrary", vmem=ATTN_VMEM_LIMIT),
        name="mla_attn",
    )(q3, kv3, kr3, kv3)
    return out.reshape(batch * seq, heads * MLA_V)


def _post_attn_kernel(o_ref, wao_ref, x_ref, g_ref, wq_ref, kv_ref, wo_ref, out_ref, *, q_scale):
    x = x_ref[...] + jnp.dot(o_ref[...], wao_ref[...], preferred_element_type=F32)
    h = _rms(x, g_ref[...]).astype(BF16)
    q = (jnp.dot(h, wq_ref[...], preferred_element_type=F32) * q_scale).astype(BF16)
    hd = CA_HEAD_DIM
    kw = CA_HEADS * hd
    outs = []
    for hh in range(CA_HEADS):
        k = kv_ref[:, hh * hd:(hh + 1) * hd]
        v = kv_ref[:, kw + hh * hd:kw + (hh + 1) * hd]
        s = lax.dot_general(q[:, hh * hd:(hh + 1) * hd], k, (((1,), (1,)), ((), ())),
                            preferred_element_type=F32)
        p = jnp.exp2(s - jnp.max(s, axis=1, keepdims=True))
        l = jnp.sum(p, axis=1, keepdims=True)
        outs.append((jnp.dot(p.astype(BF16), v, preferred_element_type=F32) / l).astype(BF16))
    o = jnp.concatenate(outs, axis=1)
    out_ref[...] = x + jnp.dot(o, wo_ref[...], preferred_element_type=F32)


def _post_attn(o, wao, x, g, wq, kv_all, layer, wo, seq, tm_target=512):
    m, d = x.shape
    tm = _pick(seq, tm_target)
    blocks_per_batch = seq // tm
    n_mem = kv_all.shape[1]
    kvw = 2 * CA_HEADS * CA_HEAD_DIM
    kern = functools.partial(_post_attn_kernel, q_scale=LOG2E / math.sqrt(CA_HEAD_DIM))
    const = lambda i: (0, 0)
    once = pl.Buffered(1)
    return pl.pallas_call(
        kern,
        grid=(m // tm,),
        in_specs=[pl.BlockSpec((tm, o.shape[1]), lambda i: (i, 0)),
                  pl.BlockSpec(wao.shape, const, pipeline_mode=once),
                  pl.BlockSpec((tm, d), lambda i: (i, 0)),
                  pl.BlockSpec((1, d), const),
                  pl.BlockSpec(wq.shape, const, pipeline_mode=once),
                  pl.BlockSpec((None, n_mem, kvw), lambda i: (i // blocks_per_batch, 0, layer)),
                  pl.BlockSpec(wo.shape, const, pipeline_mode=once)],
        out_specs=pl.BlockSpec((tm, d), lambda i: (i, 0)),
        out_shape=jax.ShapeDtypeStruct((m, d), F32),
        compiler_params=_cparams("parallel"),
        name="post_attn",
    )(o, wao, x, g.reshape(1, d), wq, kv_all, wo)


def _mlp_kernel(x_ref, g_ref, wup_ref, wdown_ref, fg_ref, out_ref, h_ref, *, final_norm):
    f = pl.program_id(1)

    @pl.when(f == 0)
    def _():
        x = x_ref[...]
        h_ref[...] = _rms(x, g_ref[...]).astype(BF16)
        out_ref[...] = x

    u = jnp.maximum(jnp.dot(h_ref[...], wup_ref[...], preferred_element_type=F32), 0.0)
    out_ref[...] += jnp.dot((u * u).astype(BF16), wdown_ref[...], preferred_element_type=F32)

    if final_norm:
        @pl.when(f == pl.num_programs(1) - 1)
        def _():
            out_ref[...] = _rms(out_ref[...], fg_ref[...])


def _mlp(x, g, wup, wdown, final_g, final_norm, tm_target=1024, tf_target=512):
    m, d = x.shape
    ff = wup.shape[1]
    tm, tf = _pick(m, tm_target), _pick(ff, tf_target)
    kern = functools.partial(_mlp_kernel, final_norm=final_norm)
    return pl.pallas_call(
        kern,
        grid=(m // tm, ff // tf),
        in_specs=[pl.BlockSpec((tm, d), lambda i, f: (i, 0)),
                  pl.BlockSpec((1, d), lambda i, f: (0, 0)),
                  pl.BlockSpec((d, tf), lambda i, f: (0, f)),
                  pl.BlockSpec((tf, d), lambda i, f: (f, 0)),
                  pl.BlockSpec((1, d), lambda i, f: (0, 0))],
        out_specs=pl.BlockSpec((tm, d), lambda i, f: (i, 0)),
        out_shape=jax.ShapeDtypeStruct((m, d), F32),
        scratch_shapes=[pltpu.VMEM((tm, d), BF16)],
        compiler_params=_cparams("parallel", "arbitrary"),
        name="mlp",
    )(x, g.reshape(1, d), wup, wdown, final_g.reshape(1, d))


def kernel(x, mem, positions, attn_norm, cross_norm, mlp_norm, mem_norm, final_norm, da_wqkv, da_lambda, da_subln, da_wo, mla_wdown, mla_q_norm, mla_kv_norm, mla_wuq, mla_wukv, mla_wo, ca_wq, ca_wkv, ca_wo, mlp_wup, mlp_wdown):
    batch, seq, d = x.shape
    depth = attn_norm.shape[0]
    n_mem = mem.shape[1]
    mla_heads = mla_wo.shape[1] // MLA_V

    tab_da = _rope_table(positions, DA_ROT)
    tab_mla = _rope_table(positions, MLA_ROPE)

    wkv_all = jnp.concatenate([ca_wkv[i].astype(BF16) for i in range(depth)], axis=1)
    kv_all = _norm_matmul(mem.reshape(batch * n_mem, d), mem_norm, wkv_all, BF16)
    kv_all = kv_all.reshape(batch, n_mem, wkv_all.shape[1])

    half = MLA_ROPE // 2

    def spread_rope(w):
        gap = jnp.zeros(w.shape[:-1] + (HALF_LANES - half,), w.dtype)
        return jnp.concatenate([w[..., :half], gap, w[..., half:], gap], axis=-1)

    xs = x.reshape(batch * seq, d)
    for i in range(depth):
        j = i // 2
        if i % 2 == 0:
            lambda_init = 0.8 - 0.6 * math.exp(-0.3 * i)
            w = da_wqkv[j].astype(BF16).reshape(d, 3, d // DA_HEAD_DIM, DA_HEAD_DIM)
            w = jnp.concatenate([_rope_lane_perm(w[:, :2], DA_ROT), w[:, 2:]], axis=1)
            qkv = _da_qkv(xs, attn_norm[i], w.reshape(d, 3 * d), tab_da)
            o = _da_attention(qkv, da_lambda[j], da_subln[j], lambda_init, batch, seq)
            wao = da_wo[j]
        else:
            q_lora = mla_q_norm.shape[1]
            wd = mla_wdown[j].astype(BF16)
            n_lat = wd.shape[1] - MLA_ROPE
            wd_pad = jnp.concatenate([wd[:, :n_lat], spread_rope(wd[:, n_lat:])], axis=1)
            wuq = mla_wuq[j].astype(BF16).reshape(q_lora, mla_heads, MLA_NOPE + MLA_ROPE)
            wuq_pad = jnp.concatenate([wuq[..., :MLA_NOPE], spread_rope(wuq[..., MLA_NOPE:])], axis=-1)
            wuq_pad = wuq_pad.reshape(q_lora, mla_heads * MLA_QK_PAD)
            q, kv, kr = _mla_proj(xs, attn_norm[i], wd_pad, mla_q_norm[j], mla_kv_norm[j], wuq_pad,
                                  mla_wukv[j].astype(BF16), tab_mla, mla_heads)
            o = _mla_attention(q, kv, kr, batch, seq, mla_heads)
            wao = mla_wo[j]
        xs = _post_attn(o, wao.astype(BF16), xs, cross_norm[i], ca_wq[i].astype(BF16), kv_all, i,
                        ca_wo[i].astype(BF16), seq)
        xs = _mlp(xs, mlp_norm[i], mlp_wup[i].astype(BF16), mlp_wdown[i].astype(BF16), final_norm,
                  final_norm=(i == depth - 1))
    return xs.reshape(batch, seq, d)
```

```python
import functools
import math

import jax
import jax.numpy as jnp
import numpy as np
from jax import lax
from jax.experimental import pallas as pl
from jax.experimental.pallas import tpu as pltpu

F32 = jnp.float32
BF16 = jnp.bfloat16

EPS = 1e-6
ROPE_THETA = 500000.0
LANES = 128
LOG2E = 1.4426950408889634
NEG_BIG = -1e30

DA_HEAD_DIM = 128
DA_ROT = DA_HEAD_DIM // 4
MLA_NOPE = 128
MLA_ROPE = 64
MLA_V = 128
MLA_QK_PAD = 256
CA_HEADS = 4
CA_HEAD_DIM = 128

VMEM_LIMIT = 56 * 1024 * 1024
ATTN_VMEM_LIMIT = 60 * 1024 * 1024


def _cparams(*sem, vmem=VMEM_LIMIT):
    return pltpu.CompilerParams(dimension_semantics=sem, vmem_limit_bytes=vmem)


def _pick(n, target):
    t = min(n, target)
    while n % t or (t % 8 and t != n):
        t -= 1
    return t


def _rms(xf, g):
    ms = jnp.mean(xf * xf, axis=-1, keepdims=True)
    return xf * lax.rsqrt(ms + EPS) * g


HALF_LANES = LANES // 2


def _rope_chunk(xc, c, s):
    return xc * c + pltpu.roll(xc, HALF_LANES, 1) * s


def _rope_lane_perm(rot):
    half = rot // 2
    perm = list(range(LANES))
    perm[half:rot], perm[HALF_LANES:HALF_LANES + half] = perm[HALF_LANES:HALF_LANES + half], perm[half:rot]
    return perm


def _da_weight_columns(d):
    perm = _rope_lane_perm(DA_ROT)
    cols = []
    for chunk in range(3 * d // LANES):
        lanes = perm if chunk < 2 * d // LANES else range(LANES)
        cols.extend(chunk * LANES + lane for lane in lanes)
    return np.asarray(cols, np.int32)


def _rope_table(positions, rot):
    half = rot // 2
    inv_freq = ROPE_THETA ** (-jnp.arange(0, rot, 2, dtype=F32) / rot)
    ang = positions.astype(F32).reshape(-1)[:, None] * inv_freq
    cos, sin = lax.optimization_barrier((jnp.cos(ang), jnp.sin(ang)))
    n = ang.shape[0]
    gap = HALF_LANES - half
    c = jnp.concatenate([cos, jnp.ones((n, gap), F32), cos, jnp.ones((n, gap), F32)], axis=1)
    s = jnp.concatenate([-sin, jnp.zeros((n, gap), F32), sin, jnp.zeros((n, gap), F32)], axis=1)
    return jnp.concatenate([c, s], axis=1)


def _norm_matmul_kernel(x_ref, g_ref, w_ref, o_ref, h_ref):
    @pl.when(pl.program_id(1) == 0)
    def _():
        h_ref[...] = _rms(x_ref[...], g_ref[...]).astype(BF16)

    o_ref[...] = jnp.dot(h_ref[...], w_ref[...], preferred_element_type=F32).astype(o_ref.dtype)


def _norm_matmul(x, g, w, out_dtype, tm_target=512, tn_target=1024):
    m, d = x.shape
    n = w.shape[1]
    tm, tn = _pick(m, tm_target), _pick(n, tn_target)
    return pl.pallas_call(
        _norm_matmul_kernel,
        grid=(m // tm, n // tn),
        in_specs=[pl.BlockSpec((tm, d), lambda i, j: (i, 0)),
                  pl.BlockSpec((1, d), lambda i, j: (0, 0)),
                  pl.BlockSpec((d, tn), lambda i, j: (0, j))],
        out_specs=pl.BlockSpec((tm, tn), lambda i, j: (i, j)),
        out_shape=jax.ShapeDtypeStruct((m, n), out_dtype),
        scratch_shapes=[pltpu.VMEM((tm, d), BF16)],
        compiler_params=_cparams("parallel", "arbitrary"),
        name="norm_matmul",
    )(x, g.reshape(1, d), w)


def _da_qkv_kernel(x_ref, g_ref, w_ref, tab_ref, o_ref, h_ref, *, n_head_blocks, q_scale):
    j = pl.program_id(1)

    @pl.when(j == 0)
    def _():
        h_ref[...] = _rms(x_ref[...], g_ref[...]).astype(BF16)

    acc = jnp.dot(h_ref[...], w_ref[...], preferred_element_type=F32)
    rotary = j < 2 * n_head_blocks
    scale = jnp.where(j < n_head_blocks, q_scale, 1.0).astype(F32)
    c = jnp.where(rotary, tab_ref[:, 0:LANES], 1.0) * scale
    s = jnp.where(rotary, tab_ref[:, LANES:2 * LANES], 0.0) * scale
    for ch in range(acc.shape[1] // LANES):
        cols = slice(ch * LANES, (ch + 1) * LANES)
        o_ref[:, cols] = _rope_chunk(acc[:, cols], c, s).astype(o_ref.dtype)


def _da_qkv(x, g, w, tab, tm_target=1024, tn_target=1024):
    m, d = x.shape
    n = w.shape[1]
    tm, tn = _pick(m, tm_target), _pick(d, tn_target)
    q_scale = LOG2E / math.sqrt(DA_HEAD_DIM)
    kern = functools.partial(_da_qkv_kernel, n_head_blocks=d // tn, q_scale=q_scale)
    return pl.pallas_call(
        kern,
        grid=(m // tm, n // tn),
        in_specs=[pl.BlockSpec((tm, d), lambda i, j: (i, 0)),
                  pl.BlockSpec((1, d), lambda i, j: (0, 0)),
                  pl.BlockSpec((d, tn), lambda i, j: (0, j)),
                  pl.BlockSpec((tm, 2 * LANES), lambda i, j: (i, 0))],
        out_specs=pl.BlockSpec((tm, tn), lambda i, j: (i, j)),
        out_shape=jax.ShapeDtypeStruct((m, n), BF16),
        scratch_shapes=[pltpu.VMEM((tm, d), BF16)],
        compiler_params=_cparams("parallel", "arbitrary"),
        name="da_qkv",
    )(x, g.reshape(1, d), w, tab)


N_BUF = 2


def _causal_sweep(qi, tk, ratio, score, softmax, accumulate):
    first = ratio * qi
    below = [slice(j * tk, ratio * tk) for j in range(ratio)]
    score(first, 0)
    softmax(0, 0)
    for j in range(1, ratio):
        score(first + j, j % N_BUF, rows=below[j])
        accumulate(first + j - 1, (j - 1) % N_BUF, rows=below[j - 1])
        softmax(j % N_BUF, 0, rows=below[j])
    accumulate(first + ratio - 1, (ratio - 1) % N_BUF, rows=below[ratio - 1])

    @pl.when(qi > 0)
    def _():
        score(0, 0)
        softmax(0, None)
        score(1, 1)

        def pair(u, carry):
            t = 2 * u
            accumulate(t, 0)
            softmax(1, None)
            score(t + 2, 0)
            accumulate(t + 1, 1)
            softmax(0, None)
            score(t + 3, 1)
            return carry

        lax.fori_loop(0, first // 2 - 1, pair, 0)
        t = first - 2
        accumulate(t, 0)
        softmax(1, None)
        accumulate(t + 1, 1)


def _softmax_update(s, diag, m_prev, l_prev):
    tq, tk = s.shape
    if diag is not None:
        row = lax.broadcasted_iota(jnp.int32, (tq, tk), 0)
        col = lax.broadcasted_iota(jnp.int32, (tq, tk), 1)
        s = jnp.where(col + diag <= row, s, NEG_BIG)
    m_next = jnp.maximum(m_prev, jnp.max(s, axis=1, keepdims=True))
    alpha = jnp.exp2(m_prev - m_next)
    p = jnp.exp2(s - jnp.tile(m_next, (1, tk // LANES)))
    part = p[:, 0:LANES]
    for ch in range(1, tk // LANES):
        part = part + p[:, ch * LANES:(ch + 1) * LANES]
    return m_next, alpha * l_prev + part, alpha, p.astype(BF16)


def _attn_stages(n_streams, tk, q_of, k_of, v_of, s_ref, p_ref, a_ref, m_ref, l_ref, acc_ref):
    nt = (((1,), (1,)), ((), ()))
    every = slice(None)

    def score(blk, par, rows=every):
        keys = pl.ds(pl.multiple_of(blk * tk, tk), tk)
        for c in range(n_streams):
            s_ref[par, c, rows] = lax.dot_general(q_of(c, rows), k_of(c, keys), nt, preferred_element_type=F32)

    def softmax(par, diag, rows=every):
        for c in range(n_streams):
            m_next, l_next, alpha, p = _softmax_update(s_ref[par, c, rows], diag, m_ref[c, rows], l_ref[c, rows])
            m_ref[c, rows] = m_next
            l_ref[c, rows] = l_next
            a_ref[par, c, rows] = alpha
            p_ref[par, c, rows] = p

    def accumulate(blk, par, rows=every):
        v = v_of(pl.ds(pl.multiple_of(blk * tk, tk), tk))
        rep = acc_ref.shape[-1] // LANES
        for c in range(n_streams):
            pv = jnp.dot(p_ref[par, c, rows], v, preferred_element_type=F32)
            acc_ref[c, rows] = acc_ref[c, rows] * jnp.tile(a_ref[par, c, rows], (1, rep)) + pv

    return score, softmax, accumulate


def _attn_scratch(n_streams, tq, tk, vw):
    return [pltpu.VMEM((N_BUF, n_streams, tq, tk), F32),
            pltpu.VMEM((N_BUF, n_streams, tq, tk), BF16),
            pltpu.VMEM((N_BUF, n_streams, tq, LANES), F32),
            pltpu.VMEM((n_streams, tq, LANES), F32),
            pltpu.VMEM((n_streams, tq, LANES), F32),
            pltpu.VMEM((n_streams, tq, vw), F32)]


def _da_attn_kernel(q_ref, k_ref, v_ref, lam_ref, sub_ref, o_ref, s_ref, p_ref, a_ref, m_ref, l_ref, acc_ref,
                    *, tk, ratio, lambda_init):
    qi = pl.program_id(2)
    hd = DA_HEAD_DIM
    m_ref[...] = jnp.full(m_ref.shape, NEG_BIG, F32)
    l_ref[...] = jnp.zeros(l_ref.shape, F32)
    acc_ref[...] = jnp.zeros(acc_ref.shape, F32)

    stages = _attn_stages(
        2, tk,
        lambda c, rows: q_ref[rows, c * hd:(c + 1) * hd],
        lambda c, keys: k_ref[keys, c * hd:(c + 1) * hd],
        lambda rows: v_ref[rows, :],
        s_ref, p_ref, a_ref, m_ref, l_ref, acc_ref)
    _causal_sweep(qi, tk, ratio, *stages)

    lv = lam_ref[...]
    lam = (jnp.exp(jnp.sum(lv[0:1] * lv[1:2], axis=1, keepdims=True))
           - jnp.exp(jnp.sum(lv[2:3] * lv[3:4], axis=1, keepdims=True)) + lambda_init)
    l0 = jnp.sum(l_ref[0], axis=1, keepdims=True)
    l1 = jnp.sum(l_ref[1], axis=1, keepdims=True)
    o = acc_ref[0] / l0 - lam * (acc_ref[1] / l1)
    o_ref[...] = (_rms(o, sub_ref[...]) * (1.0 - lambda_init)).astype(o_ref.dtype)


DA_RATIO = 2
MLA_RATIO = 4


def _attn_blocks(seq, ratio, tk_target=512):
    tk = _pick(seq // ratio, tk_target)
    assert ratio % 2 == 0 and seq % (ratio * tk) == 0 and tk % LANES == 0
    return ratio * tk, tk


def _da_attention(qkv, lam_vecs, subln, lambda_init, batch, seq):
    d3 = qkv.shape[1]
    d = d3 // 3
    hw = 2 * DA_HEAD_DIM
    heads = d // hw
    tq, tk = _attn_blocks(seq, DA_RATIO)
    qkv3 = qkv.reshape(batch, seq, d3)
    kern = functools.partial(_da_attn_kernel, tk=tk, ratio=DA_RATIO, lambda_init=lambda_init)
    resident = pl.Buffered(1)
    out = pl.pallas_call(
        kern,
        grid=(batch, heads, seq // tq),
        in_specs=[pl.BlockSpec((None, tq, hw), lambda b, h, i: (b, i, h)),
                  pl.BlockSpec((None, seq, hw), lambda b, h, i: (b, 0, heads + h), pipeline_mode=resident),
                  pl.BlockSpec((None, seq, hw), lambda b, h, i: (b, 0, 2 * heads + h), pipeline_mode=resident),
                  pl.BlockSpec((4, DA_HEAD_DIM), lambda b, h, i: (0, 0)),
                  pl.BlockSpec((1, hw), lambda b, h, i: (0, 0))],
        out_specs=pl.BlockSpec((None, tq, hw), lambda b, h, i: (b, i, h)),
        out_shape=jax.ShapeDtypeStruct((batch, seq, d), BF16),
        scratch_shapes=_attn_scratch(2, tq, tk, hw),
        compiler_params=_cparams("parallel", "parallel", "arbitrary", vmem=ATTN_VMEM_LIMIT),
        name="da_attn",
    )(qkv3, qkv3, qkv3, lam_vecs, subln.reshape(1, hw))
    return out.reshape(batch * seq, d)


def _mla_proj_kernel(x_ref, g_ref, wd_ref, qn_ref, kvn_ref, wuq_ref, wukv_ref, tab_ref,
                     q_ref, kv_ref, kr_ref, *, q_lora, kv_lora, heads, q_scale):
    h = _rms(x_ref[...], g_ref[...]).astype(BF16)
    d = jnp.dot(h, wd_ref[...], preferred_element_type=F32)
    rc = tab_ref[:, 0:LANES]
    rs = tab_ref[:, LANES:2 * LANES]
    c_q = _rms(d[:, :q_lora], qn_ref[...]).astype(BF16)
    c_kv = _rms(d[:, q_lora:q_lora + kv_lora], kvn_ref[...]).astype(BF16)
    kr = _rope_chunk(d[:, q_lora + kv_lora:q_lora + kv_lora + LANES], rc, rs)
    kr_ref[...] = kr.astype(kr_ref.dtype)
    w = MLA_QK_PAD
    qc, qs = rc * q_scale, rs * q_scale
    for hh in range(heads):
        qh = jnp.dot(c_q, wuq_ref[:, hh * w:(hh + 1) * w], preferred_element_type=F32)
        q_ref[:, hh * w:hh * w + LANES] = (qh[:, :LANES] * q_scale).astype(q_ref.dtype)
        q_ref[:, hh * w + LANES:(hh + 1) * w] = _rope_chunk(qh[:, LANES:], qc, qs).astype(q_ref.dtype)
        kv_ref[:, hh * w:(hh + 1) * w] = jnp.dot(
            c_kv, wukv_ref[:, hh * w:(hh + 1) * w], preferred_element_type=F32).astype(kv_ref.dtype)


def _mla_proj(x, g, wd_pad, qn, kvn, wuq_pad, wukv, tab, heads, tm_target=256):
    m, d = x.shape
    tm = _pick(m, tm_target)
    q_lora, kv_lora = qn.shape[0], kvn.shape[0]
    q_scale = LOG2E / math.sqrt(MLA_NOPE + MLA_ROPE)
    kern = functools.partial(_mla_proj_kernel, q_lora=q_lora, kv_lora=kv_lora, heads=heads, q_scale=q_scale)
    const = lambda i: (0, 0)
    row = lambda i: (i, 0)
    nq, nkv = wuq_pad.shape[1], wukv.shape[1]
    return pl.pallas_call(
        kern,
        grid=(m // tm,),
        in_specs=[pl.BlockSpec((tm, d), row),
                  pl.BlockSpec((1, d), const),
                  pl.BlockSpec(wd_pad.shape, const),
                  pl.BlockSpec((1, q_lora), const),
                  pl.BlockSpec((1, kv_lora), const),
                  pl.BlockSpec(wuq_pad.shape, const),
                  pl.BlockSpec(wukv.shape, const),
                  pl.BlockSpec((tm, 2 * LANES), row)],
        out_specs=[pl.BlockSpec((tm, nq), row),
                   pl.BlockSpec((tm, nkv), row),
                   pl.BlockSpec((tm, LANES), row)],
        out_shape=[jax.ShapeDtypeStruct((m, nq), BF16),
                   jax.ShapeDtypeStruct((m, nkv), BF16),
                   jax.ShapeDtypeStruct((m, LANES), BF16)],
        compiler_params=_cparams("parallel"),
        name="mla_proj",
    )(x, g.reshape(1, d), wd_pad, qn.reshape(1, q_lora), kvn.reshape(1, kv_lora), wuq_pad, wukv, tab)


def _mla_attn_kernel(q_ref, kn_ref, kr_ref, v_ref, o_ref, s_ref, p_ref, a_ref, m_ref, l_ref, acc_ref,
                     *, tk, ratio):
    qi = pl.program_id(2)
    m_ref[...] = jnp.full(m_ref.shape, NEG_BIG, F32)
    l_ref[...] = jnp.zeros(l_ref.shape, F32)
    acc_ref[...] = jnp.zeros(acc_ref.shape, F32)

    stages = _attn_stages(
        1, tk,
        lambda c, rows: q_ref[rows, :],
        lambda c, keys: jnp.concatenate([kn_ref[keys, :], kr_ref[keys, :]], axis=1),
        lambda rows: v_ref[rows, :],
        s_ref, p_ref, a_ref, m_ref, l_ref, acc_ref)
    _causal_sweep(qi, tk, ratio, *stages)
    o_ref[...] = (acc_ref[0] / jnp.sum(l_ref[0], axis=1, keepdims=True)).astype(o_ref.dtype)


def _mla_attention(q, kv, kr, batch, seq, heads):
    tq, tk = _attn_blocks(seq, MLA_RATIO)
    w = MLA_QK_PAD
    q3 = q.reshape(batch, seq, heads * w)
    kv3 = kv.reshape(batch, seq, heads * w)
    kr3 = kr.reshape(batch, seq, LANES)
    kern = functools.partial(_mla_attn_kernel, tk=tk, ratio=MLA_RATIO)
    out = pl.pallas_call(
        kern,
        grid=(batch, heads, seq // tq),
        in_specs=[pl.BlockSpec((None, tq, w), lambda b, h, i: (b, i, h)),
                  pl.BlockSpec((None, seq, LANES), lambda b, h, i: (b, 0, 2 * h)),
                  pl.BlockSpec((None, seq, LANES), lambda b, h, i: (b, 0, 0)),
                  pl.BlockSpec((None, seq, LANES), lambda b, h, i: (b, 0, 2 * h + 1))],
        out_specs=pl.BlockSpec((None, tq, MLA_V), lambda b, h, i: (b, i, h)),
        out_shape=jax.ShapeDtypeStruct((batch, seq, heads * MLA_V), BF16),
        scratch_shapes=_attn_scratch(1, tq, tk, MLA_V),
        compiler_params=_cparams("parallel", "parallel", "arbitrary", vmem=ATTN_VMEM_LIMIT),
        name="mla_attn",
    )(q3, kv3, kr3, kv3)
    return out.reshape(batch * seq, heads * MLA_V)


def _post_attn_kernel(o_ref, wao_ref, x_ref, g_ref, wq_ref, kv_ref, wo_ref, out_ref, *, q_scale):
    x = x_ref[...] + jnp.dot(o_ref[...], wao_ref[...], preferred_element_type=F32)
    h = _rms(x, g_ref[...]).astype(BF16)
    q = (jnp.dot(h, wq_ref[...], preferred_element_type=F32) * q_scale).astype(BF16)
    hd = CA_HEAD_DIM
    kw = CA_HEADS * hd
    outs = []
    for hh in range(CA_HEADS):
        k = kv_ref[:, hh * hd:(hh + 1) * hd]
        v = kv_ref[:, kw + hh * hd:kw + (hh + 1) * hd]
        s = lax.dot_general(q[:, hh * hd:(hh + 1) * hd], k, (((1,), (1,)), ((), ())),
                            preferred_element_type=F32)
        p = jnp.exp2(s - jnp.max(s, axis=1, keepdims=True))
        l = jnp.sum(p, axis=1, keepdims=True)
        outs.append((jnp.dot(p.astype(BF16), v, preferred_element_type=F32) / l).astype(BF16))
    o = jnp.concatenate(outs, axis=1)
    out_ref[...] = x + jnp.dot(o, wo_ref[...], preferred_element_type=F32)


def _post_attn(o, wao, x, g, wq, kv_all, layer, wo, seq, tm_target=512):
    m, d = x.shape
    tm = _pick(seq, tm_target)
    blocks_per_batch = seq // tm
    n_mem = kv_all.shape[1]
    kvw = 2 * CA_HEADS * CA_HEAD_DIM
    kern = functools.partial(_post_attn_kernel, q_scale=LOG2E / math.sqrt(CA_HEAD_DIM))
    const = lambda i: (0, 0)
    once = pl.Buffered(1)
    return pl.pallas_call(
        kern,
        grid=(m // tm,),
        in_specs=[pl.BlockSpec((tm, o.shape[1]), lambda i: (i, 0)),
                  pl.BlockSpec(wao.shape, const, pipeline_mode=once),
                  pl.BlockSpec((tm, d), lambda i: (i, 0)),
                  pl.BlockSpec((1, d), const),
                  pl.BlockSpec(wq.shape, const, pipeline_mode=once),
                  pl.BlockSpec((None, n_mem, kvw), lambda i: (i // blocks_per_batch, 0, layer)),
                  pl.BlockSpec(wo.shape, const, pipeline_mode=once)],
        out_specs=pl.BlockSpec((tm, d), lambda i: (i, 0)),
        out_shape=jax.ShapeDtypeStruct((m, d), F32),
        compiler_params=_cparams("parallel"),
        name="post_attn",
    )(o, wao, x, g.reshape(1, d), wq, kv_all, wo)


def _mlp_kernel(x_ref, g_ref, wup_ref, wdown_ref, fg_ref, out_ref, h_ref, *, final_norm):
    f = pl.program_id(1)

    @pl.when(f == 0)
    def _():
        x = x_ref[...]
        h_ref[...] = _rms(x, g_ref[...]).astype(BF16)
        out_ref[...] = x

    u = jnp.maximum(jnp.dot(h_ref[...], wup_ref[...], preferred_element_type=F32), 0.0)
    out_ref[...] += jnp.dot((u * u).astype(BF16), wdown_ref[...], preferred_element_type=F32)

    if final_norm:
        @pl.when(f == pl.num_programs(1) - 1)
        def _():
            out_ref[...] = _rms(out_ref[...], fg_ref[...])


def _mlp(x, g, wup, wdown, final_g, final_norm, tm_target=1024, tf_target=512):
    m, d = x.shape
    ff = wup.shape[1]
    tm, tf = _pick(m, tm_target), _pick(ff, tf_target)
    kern = functools.partial(_mlp_kernel, final_norm=final_norm)
    return pl.pallas_call(
        kern,
        grid=(m // tm, ff // tf),
        in_specs=[pl.BlockSpec((tm, d), lambda i, f: (i, 0)),
                  pl.BlockSpec((1, d), lambda i, f: (0, 0)),
                  pl.BlockSpec((d, tf), lambda i, f: (0, f)),
                  pl.BlockSpec((tf, d), lambda i, f: (f, 0)),
                  pl.BlockSpec((1, d), lambda i, f: (0, 0))],
        out_specs=pl.BlockSpec((tm, d), lambda i, f: (i, 0)),
        out_shape=jax.ShapeDtypeStruct((m, d), F32),
        scratch_shapes=[pltpu.VMEM((tm, d), BF16)],
        compiler_params=_cparams("parallel", "arbitrary"),
        name="mlp",
    )(x, g.reshape(1, d), wup, wdown, final_g.reshape(1, d))


def kernel(x, mem, positions, attn_norm, cross_norm, mlp_norm, mem_norm, final_norm, da_wqkv, da_lambda, da_subln, da_wo, mla_wdown, mla_q_norm, mla_kv_norm, mla_wuq, mla_wukv, mla_wo, ca_wq, ca_wkv, ca_wo, mlp_wup, mlp_wdown):
    batch, seq, d = x.shape
    depth = attn_norm.shape[0]
    n_mem = mem.shape[1]
    mla_heads = mla_wo.shape[1] // MLA_V

    tab_da = _rope_table(positions, DA_ROT)
    tab_mla = _rope_table(positions, MLA_ROPE)

    wkv_all = jnp.concatenate([ca_wkv[i].astype(BF16) for i in range(depth)], axis=1)
    kv_all = _norm_matmul(mem.reshape(batch * n_mem, d), mem_norm, wkv_all, BF16)
    kv_all = kv_all.reshape(batch, n_mem, wkv_all.shape[1])

    half = MLA_ROPE // 2

    def spread_rope(w):
        gap = jnp.zeros(w.shape[:-1] + (HALF_LANES - half,), w.dtype)
        return jnp.concatenate([w[..., :half], gap, w[..., half:], gap], axis=-1)

    xs = x.reshape(batch * seq, d)
    for i in range(depth):
        j = i // 2
        if i % 2 == 0:
            lambda_init = 0.8 - 0.6 * math.exp(-0.3 * i)
            w = jnp.take(da_wqkv[j], _da_weight_columns(d), axis=1).astype(BF16)
            qkv = _da_qkv(xs, attn_norm[i], w, tab_da)
            o = _da_attention(qkv, da_lambda[j], da_subln[j], lambda_init, batch, seq)
            wao = da_wo[j]
        else:
            q_lora = mla_q_norm.shape[1]
            wd = mla_wdown[j].astype(BF16)
            n_lat = wd.shape[1] - MLA_ROPE
            wd_pad = jnp.concatenate([wd[:, :n_lat], spread_rope(wd[:, n_lat:])], axis=1)
            wuq = mla_wuq[j].astype(BF16).reshape(q_lora, mla_heads, MLA_NOPE + MLA_ROPE)
            wuq_pad = jnp.concatenate([wuq[..., :MLA_NOPE], spread_rope(wuq[..., MLA_NOPE:])], axis=-1)
            wuq_pad = wuq_pad.reshape(q_lora, mla_heads * MLA_QK_PAD)
            q, kv, kr = _mla_proj(xs, attn_norm[i], wd_pad, mla_q_norm[j], mla_kv_norm[j], wuq_pad,
                                  mla_wukv[j].astype(BF16), tab_mla, mla_heads)
            o = _mla_attention(q, kv, kr, batch, seq, mla_heads)
            wao = mla_wo[j]
        xs = _post_attn(o, wao.astype(BF16), xs, cross_norm[i], ca_wq[i].astype(BF16), kv_all, i,
                        ca_wo[i].astype(BF16), seq)
        xs = _mlp(xs, mlp_norm[i], mlp_wup[i].astype(BF16), mlp_wdown[i].astype(BF16), final_norm,
                  final_norm=(i == depth - 1))
    return xs.reshape(batch, seq, d)
```

```python
import functools
import math

import jax
import jax.numpy as jnp
import numpy as np
from jax import lax
from jax.experimental import pallas as pl
from jax.experimental.pallas import tpu as pltpu

F32 = jnp.float32
BF16 = jnp.bfloat16

EPS = 1e-6
ROPE_THETA = 500000.0
LANES = 128
LOG2E = 1.4426950408889634
NEG_BIG = -1e30

DA_HEAD_DIM = 128
DA_ROT = DA_HEAD_DIM // 4
MLA_NOPE = 128
MLA_ROPE = 64
MLA_V = 128
MLA_QK_PAD = 256
CA_HEADS = 4
CA_HEAD_DIM = 128

VMEM_LIMIT = 56 * 1024 * 1024
ATTN_VMEM_LIMIT = 60 * 1024 * 1024


def _cparams(*sem, vmem=VMEM_LIMIT):
    return pltpu.CompilerParams(dimension_semantics=sem, vmem_limit_bytes=vmem)


def _pick(n, target):
    t = min(n, target)
    while n % t or (t % 8 and t != n):
        t -= 1
    return t


def _rms(xf, g):
    ms = jnp.mean(xf * xf, axis=-1, keepdims=True)
    return xf * lax.rsqrt(ms + EPS) * g


HALF_LANES = LANES // 2


def _rope_chunk(xc, c, s):
    return xc * c + pltpu.roll(xc, HALF_LANES, 1) * s


def _rope_lane_perm(rot):
    half = rot // 2
    perm = list(range(LANES))
    perm[half:rot], perm[HALF_LANES:HALF_LANES + half] = perm[HALF_LANES:HALF_LANES + half], perm[half:rot]
    return perm


def _da_weight_columns(d):
    perm = _rope_lane_perm(DA_ROT)
    cols = []
    for chunk in range(3 * d // LANES):
        lanes = perm if chunk < 2 * d // LANES else range(LANES)
        cols.extend(chunk * LANES + lane for lane in lanes)
    return np.asarray(cols, np.int32)


def _rope_table(positions, rot):
    half = rot // 2
    inv_freq = ROPE_THETA ** (-jnp.arange(0, rot, 2, dtype=F32) / rot)
    ang = positions.astype(F32).reshape(-1)[:, None] * inv_freq
    cos, sin = jnp.cos(ang), jnp.sin(ang)
    n = ang.shape[0]
    src = jnp.concatenate([cos, sin, -sin, jnp.ones((n, 1), F32), jnp.zeros((n, 1), F32)], axis=1)
    one, zero = 3 * half, 3 * half + 1
    c_idx = [j if j < half else one for j in range(HALF_LANES)] * 2
    s_idx = ([2 * half + j if j < half else zero for j in range(HALF_LANES)]
             + [half + j if j < half else zero for j in range(HALF_LANES)])
    return jnp.take(src, np.asarray(c_idx + s_idx, np.int32), axis=1)


def _norm_matmul_kernel(x_ref, g_ref, w_ref, o_ref, h_ref):
    @pl.when(pl.program_id(1) == 0)
    def _():
        h_ref[...] = _rms(x_ref[...], g_ref[...]).astype(BF16)

    o_ref[...] = jnp.dot(h_ref[...], w_ref[...], preferred_element_type=F32).astype(o_ref.dtype)


def _norm_matmul(x, g, w, out_dtype, tm_target=512, tn_target=1024):
    m, d = x.shape
    n = w.shape[1]
    tm, tn = _pick(m, tm_target), _pick(n, tn_target)
    return pl.pallas_call(
        _norm_matmul_kernel,
        grid=(m // tm, n // tn),
        in_specs=[pl.BlockSpec((tm, d), lambda i, j: (i, 0)),
                  pl.BlockSpec((1, d), lambda i, j: (0, 0)),
                  pl.BlockSpec((d, tn), lambda i, j: (0, j))],
        out_specs=pl.BlockSpec((tm, tn), lambda i, j: (i, j)),
        out_shape=jax.ShapeDtypeStruct((m, n), out_dtype),
        scratch_shapes=[pltpu.VMEM((tm, d), BF16)],
        compiler_params=_cparams("parallel", "arbitrary"),
        name="norm_matmul",
    )(x, g.reshape(1, d), w)


def _da_qkv_kernel(x_ref, g_ref, w_ref, tab_ref, o_ref, h_ref, *, n_head_blocks, q_scale):
    j = pl.program_id(1)

    @pl.when(j == 0)
    def _():
        h_ref[...] = _rms(x_ref[...], g_ref[...]).astype(BF16)

    acc = jnp.dot(h_ref[...], w_ref[...], preferred_element_type=F32)
    rotary = j < 2 * n_head_blocks
    scale = jnp.where(j < n_head_blocks, q_scale, 1.0).astype(F32)
    c = jnp.where(rotary, tab_ref[:, 0:LANES], 1.0) * scale
    s = jnp.where(rotary, tab_ref[:, LANES:2 * LANES], 0.0) * scale
    for ch in range(acc.shape[1] // LANES):
        cols = slice(ch * LANES, (ch + 1) * LANES)
        o_ref[:, cols] = _rope_chunk(acc[:, cols], c, s).astype(o_ref.dtype)


def _da_qkv(x, g, w, tab, tm_target=1024, tn_target=1024):
    m, d = x.shape
    n = w.shape[1]
    tm, tn = _pick(m, tm_target), _pick(d, tn_target)
    q_scale = LOG2E / math.sqrt(DA_HEAD_DIM)
    kern = functools.partial(_da_qkv_kernel, n_head_blocks=d // tn, q_scale=q_scale)
    return pl.pallas_call(
        kern,
        grid=(m // tm, n // tn),
        in_specs=[pl.BlockSpec((tm, d), lambda i, j: (i, 0)),
                  pl.BlockSpec((1, d), lambda i, j: (0, 0)),
                  pl.BlockSpec((d, tn), lambda i, j: (0, j)),
                  pl.BlockSpec((tm, 2 * LANES), lambda i, j: (i, 0))],
        out_specs=pl.BlockSpec((tm, tn), lambda i, j: (i, j)),
        out_shape=jax.ShapeDtypeStruct((m, n), BF16),
        scratch_shapes=[pltpu.VMEM((tm, d), BF16)],
        compiler_params=_cparams("parallel", "arbitrary"),
        name="da_qkv",
    )(x, g.reshape(1, d), w, tab)


N_BUF = 2


def _causal_sweep(qi, tk, ratio, score, softmax, accumulate):
    first = ratio * qi
    below = [slice(j * tk, ratio * tk) for j in range(ratio)]
    score(first, 0)
    softmax(0, 0)
    for j in range(1, ratio):
        score(first + j, j % N_BUF, rows=below[j])
        accumulate(first + j - 1, (j - 1) % N_BUF, rows=below[j - 1])
        softmax(j % N_BUF, 0, rows=below[j])
    accumulate(first + ratio - 1, (ratio - 1) % N_BUF, rows=below[ratio - 1])

    @pl.when(qi > 0)
    def _():
        score(0, 0)
        softmax(0, None)
        score(1, 1)

        def pair(u, carry):
            t = 2 * u
            accumulate(t, 0)
            softmax(1, None)
            score(t + 2, 0)
            accumulate(t + 1, 1)
            softmax(0, None)
            score(t + 3, 1)
            return carry

        lax.fori_loop(0, first // 2 - 1, pair, 0)
        t = first - 2
        accumulate(t, 0)
        softmax(1, None)
        accumulate(t + 1, 1)


def _softmax_update(s, diag, m_prev, l_prev):
    tq, tk = s.shape
    if diag is not None:
        row = lax.broadcasted_iota(jnp.int32, (tq, tk), 0)
        col = lax.broadcasted_iota(jnp.int32, (tq, tk), 1)
        s = jnp.where(col + diag <= row, s, NEG_BIG)
    m_next = jnp.maximum(m_prev, jnp.max(s, axis=1, keepdims=True))
    alpha = jnp.exp2(m_prev - m_next)
    p = jnp.exp2(s - jnp.tile(m_next, (1, tk // LANES)))
    part = p[:, 0:LANES]
    for ch in range(1, tk // LANES):
        part = part + p[:, ch * LANES:(ch + 1) * LANES]
    return m_next, alpha * l_prev + part, alpha, p.astype(BF16)


def _attn_stages(n_streams, tk, q_of, k_of, v_of, s_ref, p_ref, a_ref, m_ref, l_ref, acc_ref):
    nt = (((1,), (1,)), ((), ()))
    every = slice(None)

    def score(blk, par, rows=every):
        keys = pl.ds(pl.multiple_of(blk * tk, tk), tk)
        for c in range(n_streams):
            s_ref[par, c, rows] = lax.dot_general(q_of(c, rows), k_of(c, keys), nt, preferred_element_type=F32)

    def softmax(par, diag, rows=every):
        for c in range(n_streams):
            m_next, l_next, alpha, p = _softmax_update(s_ref[par, c, rows], diag, m_ref[c, rows], l_ref[c, rows])
            m_ref[c, rows] = m_next
            l_ref[c, rows] = l_next
            a_ref[par, c, rows] = alpha
            p_ref[par, c, rows] = p

    def accumulate(blk, par, rows=every):
        v = v_of(pl.ds(pl.multiple_of(blk * tk, tk), tk))
        rep = acc_ref.shape[-1] // LANES
        for c in range(n_streams):
            pv = jnp.dot(p_ref[par, c, rows], v, preferred_element_type=F32)
            acc_ref[c, rows] = acc_ref[c, rows] * jnp.tile(a_ref[par, c, rows], (1, rep)) + pv

    return score, softmax, accumulate


def _attn_scratch(n_streams, tq, tk, vw):
    return [pltpu.VMEM((N_BUF, n_streams, tq, tk), F32),
            pltpu.VMEM((N_BUF, n_streams, tq, tk), BF16),
            pltpu.VMEM((N_BUF, n_streams, tq, LANES), F32),
            pltpu.VMEM((n_streams, tq, LANES), F32),
            pltpu.VMEM((n_streams, tq, LANES), F32),
            pltpu.VMEM((n_streams, tq, vw), F32)]


def _da_attn_kernel(q_ref, k_ref, v_ref, lam_ref, sub_ref, o_ref, s_ref, p_ref, a_ref, m_ref, l_ref, acc_ref,
                    *, tk, ratio, lambda_init):
    qi = pl.program_id(2)
    hd = DA_HEAD_DIM
    m_ref[...] = jnp.full(m_ref.shape, NEG_BIG, F32)
    l_ref[...] = jnp.zeros(l_ref.shape, F32)
    acc_ref[...] = jnp.zeros(acc_ref.shape, F32)

    stages = _attn_stages(
        2, tk,
        lambda c, rows: q_ref[rows, c * hd:(c + 1) * hd],
        lambda c, keys: k_ref[keys, c * hd:(c + 1) * hd],
        lambda rows: v_ref[rows, :],
        s_ref, p_ref, a_ref, m_ref, l_ref, acc_ref)
    _causal_sweep(qi, tk, ratio, *stages)

    lv = lam_ref[...]
    lam = (jnp.exp(jnp.sum(lv[0:1] * lv[1:2], axis=1, keepdims=True))
           - jnp.exp(jnp.sum(lv[2:3] * lv[3:4], axis=1, keepdims=True)) + lambda_init)
    l0 = jnp.sum(l_ref[0], axis=1, keepdims=True)
    l1 = jnp.sum(l_ref[1], axis=1, keepdims=True)
    o = acc_ref[0] / l0 - lam * (acc_ref[1] / l1)
    o_ref[...] = (_rms(o, sub_ref[...]) * (1.0 - lambda_init)).astype(o_ref.dtype)


DA_RATIO = 2
MLA_RATIO = 4


def _attn_blocks(seq, ratio, tk_target=512):
    tk = _pick(seq // ratio, tk_target)
    assert ratio % 2 == 0 and seq % (ratio * tk) == 0 and tk % LANES == 0
    return ratio * tk, tk


def _da_attention(qkv, lam_vecs, subln, lambda_init, batch, seq):
    d3 = qkv.shape[1]
    d = d3 // 3
    hw = 2 * DA_HEAD_DIM
    heads = d // hw
    tq, tk = _attn_blocks(seq, DA_RATIO)
    qkv3 = qkv.reshape(batch, seq, d3)
    kern = functools.partial(_da_attn_kernel, tk=tk, ratio=DA_RATIO, lambda_init=lambda_init)
    resident = pl.Buffered(1)
    out = pl.pallas_call(
        kern,
        grid=(batch, heads, seq // tq),
        in_specs=[pl.BlockSpec((None, tq, hw), lambda b, h, i: (b, i, h)),
                  pl.BlockSpec((None, seq, hw), lambda b, h, i: (b, 0, heads + h), pipeline_mode=resident),
                  pl.BlockSpec((None, seq, hw), lambda b, h, i: (b, 0, 2 * heads + h), pipeline_mode=resident),
                  pl.BlockSpec((4, DA_HEAD_DIM), lambda b, h, i: (0, 0)),
                  pl.BlockSpec((1, hw), lambda b, h, i: (0, 0))],
        out_specs=pl.BlockSpec((None, tq, hw), lambda b, h, i: (b, i, h)),
        out_shape=jax.ShapeDtypeStruct((batch, seq, d), BF16),
        scratch_shapes=_attn_scratch(2, tq, tk, hw),
        compiler_params=_cparams("parallel", "parallel", "arbitrary", vmem=ATTN_VMEM_LIMIT),
        name="da_attn",
    )(qkv3, qkv3, qkv3, lam_vecs, subln.reshape(1, hw))
    return out.reshape(batch * seq, d)


def _mla_proj_kernel(x_ref, g_ref, wd_ref, qn_ref, kvn_ref, wuq_ref, wukv_ref, tab_ref,
                     q_ref, kv_ref, kr_ref, *, q_lora, kv_lora, heads, q_scale):
    h = _rms(x_ref[...], g_ref[...]).astype(BF16)
    d = jnp.dot(h, wd_ref[...], preferred_element_type=F32)
    rc = tab_ref[:, 0:LANES]
    rs = tab_ref[:, LANES:2 * LANES]
    c_q = _rms(d[:, :q_lora], qn_ref[...]).astype(BF16)
    c_kv = _rms(d[:, q_lora:q_lora + kv_lora], kvn_ref[...]).astype(BF16)
    kr = _rope_chunk(d[:, q_lora + kv_lora:q_lora + kv_lora + LANES], rc, rs)
    kr_ref[...] = kr.astype(kr_ref.dtype)
    w = MLA_QK_PAD
    qc, qs = rc * q_scale, rs * q_scale
    for hh in range(heads):
        qh = jnp.dot(c_q, wuq_ref[:, hh * w:(hh + 1) * w], preferred_element_type=F32)
        q_ref[:, hh * w:hh * w + LANES] = (qh[:, :LANES] * q_scale).astype(q_ref.dtype)
        q_ref[:, hh * w + LANES:(hh + 1) * w] = _rope_chunk(qh[:, LANES:], qc, qs).astype(q_ref.dtype)
        kv_ref[:, hh * w:(hh + 1) * w] = jnp.dot(
            c_kv, wukv_ref[:, hh * w:(hh + 1) * w], preferred_element_type=F32).astype(kv_ref.dtype)


def _mla_proj(x, g, wd_pad, qn, kvn, wuq_pad, wukv, tab, heads, tm_target=256):
    m, d = x.shape
    tm = _pick(m, tm_target)
    q_lora, kv_lora = qn.shape[0], kvn.shape[0]
    q_scale = LOG2E / math.sqrt(MLA_NOPE + MLA_ROPE)
    kern = functools.partial(_mla_proj_kernel, q_lora=q_lora, kv_lora=kv_lora, heads=heads, q_scale=q_scale)
    const = lambda i: (0, 0)
    row = lambda i: (i, 0)
    nq, nkv = wuq_pad.shape[1], wukv.shape[1]
    return pl.pallas_call(
        kern,
        grid=(m // tm,),
        in_specs=[pl.BlockSpec((tm, d), row),
                  pl.BlockSpec((1, d), const),
                  pl.BlockSpec(wd_pad.shape, const),
                  pl.BlockSpec((1, q_lora), const),
                  pl.BlockSpec((1, kv_lora), const),
                  pl.BlockSpec(wuq_pad.shape, const),
                  pl.BlockSpec(wukv.shape, const),
                  pl.BlockSpec((tm, 2 * LANES), row)],
        out_specs=[pl.BlockSpec((tm, nq), row),
                   pl.BlockSpec((tm, nkv), row),
                   pl.BlockSpec((tm, LANES), row)],
        out_shape=[jax.ShapeDtypeStruct((m, nq), BF16),
                   jax.ShapeDtypeStruct((m, nkv), BF16),
                   jax.ShapeDtypeStruct((m, LANES), BF16)],
        compiler_params=_cparams("parallel"),
        name="mla_proj",
    )(x, g.reshape(1, d), wd_pad, qn.reshape(1, q_lora), kvn.reshape(1, kv_lora), wuq_pad, wukv, tab)


def _mla_attn_kernel(q_ref, kn_ref, kr_ref, v_ref, o_ref, s_ref, p_ref, a_ref, m_ref, l_ref, acc_ref,
                     *, tk, ratio):
    qi = pl.program_id(2)
    m_ref[...] = jnp.full(m_ref.shape, NEG_BIG, F32)
    l_ref[...] = jnp.zeros(l_ref.shape, F32)
    acc_ref[...] = jnp.zeros(acc_ref.shape, F32)

    stages = _attn_stages(
        1, tk,
        lambda c, rows: q_ref[rows, :],
        lambda c, keys: jnp.concatenate([kn_ref[keys, :], kr_ref[keys, :]], axis=1),
        lambda rows: v_ref[rows, :],
        s_ref, p_ref, a_ref, m_ref, l_ref, acc_ref)
    _causal_sweep(qi, tk, ratio, *stages)
    o_ref[...] = (acc_ref[0] / jnp.sum(l_ref[0], axis=1, keepdims=True)).astype(o_ref.dtype)


def _mla_attention(q, kv, kr, batch, seq, heads):
    tq, tk = _attn_blocks(seq, MLA_RATIO)
    w = MLA_QK_PAD
    q3 = q.reshape(batch, seq, heads * w)
    kv3 = kv.reshape(batch, seq, heads * w)
    kr3 = kr.reshape(batch, seq, LANES)
    kern = functools.partial(_mla_attn_kernel, tk=tk, ratio=MLA_RATIO)
    out = pl.pallas_call(
        kern,
        grid=(batch, heads, seq // tq),
        in_specs=[pl.BlockSpec((None, tq, w), lambda b, h, i: (b, i, h)),
                  pl.BlockSpec((None, seq, LANES), lambda b, h, i: (b, 0, 2 * h)),
                  pl.BlockSpec((None, seq, LANES), lambda b, h, i: (b, 0, 0)),
                  pl.BlockSpec((None, seq, LANES), lambda b, h, i: (b, 0, 2 * h + 1))],
        out_specs=pl.BlockSpec((None, tq, MLA_V), lambda b, h, i: (b, i, h)),
        out_shape=jax.ShapeDtypeStruct((batch, seq, heads * MLA_V), BF16),
        scratch_shapes=_attn_scratch(1, tq, tk, MLA_V),
        compiler_params=_cparams("parallel", "parallel", "arbitrary", vmem=ATTN_VMEM_LIMIT),
        name="mla_attn",
    )(q3, kv3, kr3, kv3)
    return out.reshape(batch * seq, heads * MLA_V)


def _post_attn_kernel(o_ref, wao_ref, x_ref, g_ref, wq_ref, kv_ref, wo_ref, out_ref, *, q_scale):
    x = x_ref[...] + jnp.dot(o_ref[...], wao_ref[...], preferred_element_type=F32)
    h = _rms(x, g_ref[...]).astype(BF16)
    q = (jnp.dot(h, wq_ref[...], preferred_element_type=F32) * q_scale).astype(BF16)
    hd = CA_HEAD_DIM
    kw = CA_HEADS * hd
    outs = []
    for hh in range(CA_HEADS):
        k = kv_ref[:, hh * hd:(hh + 1) * hd]
        v = kv_ref[:, kw + hh * hd:kw + (hh + 1) * hd]
        s = lax.dot_general(q[:, hh * hd:(hh + 1) * hd], k, (((1,), (1,)), ((), ())),
                            preferred_element_type=F32)
        p = jnp.exp2(s - jnp.max(s, axis=1, keepdims=True))
        l = jnp.sum(p, axis=1, keepdims=True)
        outs.append((jnp.dot(p.astype(BF16), v, preferred_element_type=F32) / l).astype(BF16))
    o = jnp.concatenate(outs, axis=1)
    out_ref[...] = x + jnp.dot(o, wo_ref[...], preferred_element_type=F32)


def _post_attn(o, wao, x, g, wq, kv_all, layer, wo, seq, tm_target=512):
    m, d = x.shape
    tm = _pick(seq, tm_target)
    blocks_per_batch = seq // tm
    n_mem = kv_all.shape[1]
    kvw = 2 * CA_HEADS * CA_HEAD_DIM
    kern = functools.partial(_post_attn_kernel, q_scale=LOG2E / math.sqrt(CA_HEAD_DIM))
    const = lambda i: (0, 0)
    once = pl.Buffered(1)
    return pl.pallas_call(
        kern,
        grid=(m // tm,),
        in_specs=[pl.BlockSpec((tm, o.shape[1]), lambda i: (i, 0)),
                  pl.BlockSpec(wao.shape, const, pipeline_mode=once),
                  pl.BlockSpec((tm, d), lambda i: (i, 0)),
                  pl.BlockSpec((1, d), const),
                  pl.BlockSpec(wq.shape, const, pipeline_mode=once),
                  pl.BlockSpec((None, n_mem, kvw), lambda i: (i // blocks_per_batch, 0, layer)),
                  pl.BlockSpec(wo.shape, const, pipeline_mode=once)],
        out_specs=pl.BlockSpec((tm, d), lambda i: (i, 0)),
        out_shape=jax.ShapeDtypeStruct((m, d), F32),
        compiler_params=_cparams("parallel"),
        name="post_attn",
    )(o, wao, x, g.reshape(1, d), wq, kv_all, wo)


def _mlp_kernel(x_ref, g_ref, wup_ref, wdown_ref, fg_ref, out_ref, h_ref, *, final_norm):
    f = pl.program_id(1)

    @pl.when(f == 0)
    def _():
        x = x_ref[...]
        h_ref[...] = _rms(x, g_ref[...]).astype(BF16)
        out_ref[...] = x

    u = jnp.maximum(jnp.dot(h_ref[...], wup_ref[...], preferred_element_type=F32), 0.0)
    out_ref[...] += jnp.dot((u * u).astype(BF16), wdown_ref[...], preferred_element_type=F32)

    if final_norm:
        @pl.when(f == pl.num_programs(1) - 1)
        def _():
            out_ref[...] = _rms(out_ref[...], fg_ref[...])


def _mlp(x, g, wup, wdown, final_g, final_norm, tm_target=1024, tf_target=512):
    m, d = x.shape
    ff = wup.shape[1]
    tm, tf = _pick(m, tm_target), _pick(ff, tf_target)
    kern = functools.partial(_mlp_kernel, final_norm=final_norm)
    return pl.pallas_call(
        kern,
        grid=(m // tm, ff // tf),
        in_specs=[pl.BlockSpec((tm, d), lambda i, f: (i, 0)),
                  pl.BlockSpec((1, d), lambda i, f: (0, 0)),
                  pl.BlockSpec((d, tf), lambda i, f: (0, f)),
                  pl.BlockSpec((tf, d), lambda i, f: (f, 0)),
                  pl.BlockSpec((1, d), lambda i, f: (0, 0))],
        out_specs=pl.BlockSpec((tm, d), lambda i, f: (i, 0)),
        out_shape=jax.ShapeDtypeStruct((m, d), F32),
        scratch_shapes=[pltpu.VMEM((tm, d), BF16)],
        compiler_params=_cparams("parallel", "arbitrary"),
        name="mlp",
    )(x, g.reshape(1, d), wup, wdown, final_g.reshape(1, d))


def kernel(x, mem, positions, attn_norm, cross_norm, mlp_norm, mem_norm, final_norm, da_wqkv, da_lambda, da_subln, da_wo, mla_wdown, mla_q_norm, mla_kv_norm, mla_wuq, mla_wukv, mla_wo, ca_wq, ca_wkv, ca_wo, mlp_wup, mlp_wdown):
    batch, seq, d = x.shape
    depth = attn_norm.shape[0]
    n_mem = mem.shape[1]
    mla_heads = mla_wo.shape[1] // MLA_V

    tab_da = _rope_table(positions, DA_ROT)
    tab_mla = _rope_table(positions, MLA_ROPE)

    wkv_all = jnp.concatenate([ca_wkv[i].astype(BF16) for i in range(depth)], axis=1)
    kv_all = _norm_matmul(mem.reshape(batch * n_mem, d), mem_norm, wkv_all, BF16)
    kv_all = kv_all.reshape(batch, n_mem, wkv_all.shape[1])

    half = MLA_ROPE // 2

    def spread_rope(w):
        gap = jnp.zeros(w.shape[:-1] + (HALF_LANES - half,), w.dtype)
        return jnp.concatenate([w[..., :half], gap, w[..., half:], gap], axis=-1)

    xs = x.reshape(batch * seq, d)
    for i in range(depth):
        j = i // 2
        if i % 2 == 0:
            lambda_init = 0.8 - 0.6 * math.exp(-0.3 * i)
            w = jnp.take(da_wqkv[j], _da_weight_columns(d), axis=1).astype(BF16)
            qkv = _da_qkv(xs, attn_norm[i], w, tab_da)
            o = _da_attention(qkv, da_lambda[j], da_subln[j], lambda_init, batch, seq)
            wao = da_wo[j]
        else:
            q_lora = mla_q_norm.shape[1]
            wd = mla_wdown[j].astype(BF16)
            n_lat = wd.shape[1] - MLA_ROPE
            wd_pad = jnp.concatenate([wd[:, :n_lat], spread_rope(wd[:, n_lat:])], axis=1)
            wuq = mla_wuq[j].astype(BF16).reshape(q_lora, mla_heads, MLA_NOPE + MLA_ROPE)
            wuq_pad = jnp.concatenate([wuq[..., :MLA_NOPE], spread_rope(wuq[..., MLA_NOPE:])], axis=-1)
            wuq_pad = wuq_pad.reshape(q_lora, mla_heads * MLA_QK_PAD)
            q, kv, kr = _mla_proj(xs, attn_norm[i], wd_pad, mla_q_norm[j], mla_kv_norm[j], wuq_pad,
                                  mla_wukv[j].astype(BF16), tab_mla, mla_heads)
            o = _mla_attention(q, kv, kr, batch, seq, mla_heads)
            wao = mla_wo[j]
        xs = _post_attn(o, wao.astype(BF16), xs, cross_norm[i], ca_wq[i].astype(BF16), kv_all, i,
                        ca_wo[i].astype(BF16), seq)
        xs = _mlp(xs, mlp_norm[i], mlp_wup[i].astype(BF16), mlp_wdown[i].astype(BF16), final_norm,
                  final_norm=(i == depth - 1))
    return xs.reshape(batch, seq, d)
```

```python
import functools
import math

import jax
import jax.numpy as jnp
import numpy as np
from jax import lax
from jax.experimental import pallas as pl
from jax.experimental.pallas import tpu as pltpu

F32 = jnp.float32
BF16 = jnp.bfloat16

EPS = 1e-6
ROPE_THETA = 500000.0
LANES = 128
LOG2E = 1.4426950408889634
NEG_BIG = -1e30

DA_HEAD_DIM = 128
DA_ROT = DA_HEAD_DIM // 4
MLA_NOPE = 128
MLA_ROPE = 64
MLA_V = 128
MLA_QK_PAD = 256
CA_HEADS = 4
CA_HEAD_DIM = 128

VMEM_LIMIT = 56 * 1024 * 1024
ATTN_VMEM_LIMIT = 60 * 1024 * 1024


def _cparams(*sem, vmem=VMEM_LIMIT):
    return pltpu.CompilerParams(dimension_semantics=sem, vmem_limit_bytes=vmem)


def _pick(n, target):
    t = min(n, target)
    while n % t or (t % 8 and t != n):
        t -= 1
    return t


def _rms(xf, g):
    ms = jnp.mean(xf * xf, axis=-1, keepdims=True)
    return xf * lax.rsqrt(ms + EPS) * g


HALF_LANES = LANES // 2


def _rope_chunk(xc, c, s):
    return xc * c + pltpu.roll(xc, HALF_LANES, 1) * s


def _rope_lane_perm(rot):
    half = rot // 2
    perm = list(range(LANES))
    perm[half:rot], perm[HALF_LANES:HALF_LANES + half] = perm[HALF_LANES:HALF_LANES + half], perm[half:rot]
    return perm


def _da_weight_columns(d):
    perm = _rope_lane_perm(DA_ROT)
    cols = []
    for chunk in range(3 * d // LANES):
        lanes = perm if chunk < 2 * d // LANES else range(LANES)
        cols.extend(chunk * LANES + lane for lane in lanes)
    return np.asarray(cols, np.int32)


def _rope_table(positions, rot):
    half = rot // 2
    inv_freq = ROPE_THETA ** (-jnp.arange(0, rot, 2, dtype=F32) / rot)
    ang = positions.astype(F32).reshape(-1)[:, None] * inv_freq
    cos, sin = jnp.cos(ang), jnp.sin(ang)
    n = ang.shape[0]
    src = jnp.concatenate([cos, sin, -sin, jnp.ones((n, 1), F32), jnp.zeros((n, 1), F32)], axis=1)
    one, zero = 3 * half, 3 * half + 1
    c_idx = [j if j < half else one for j in range(HALF_LANES)] * 2
    s_idx = ([2 * half + j if j < half else zero for j in range(HALF_LANES)]
             + [half + j if j < half else zero for j in range(HALF_LANES)])
    return jnp.take(src, np.asarray(c_idx + s_idx, np.int32), axis=1)


def _norm_matmul_kernel(x_ref, g_ref, w_ref, o_ref, h_ref):
    @pl.when(pl.program_id(1) == 0)
    def _():
        h_ref[...] = _rms(x_ref[...], g_ref[...]).astype(BF16)

    o_ref[...] = jnp.dot(h_ref[...], w_ref[...], preferred_element_type=F32).astype(o_ref.dtype)


def _norm_matmul(x, g, w, out_dtype, tm_target=512, tn_target=1024):
    m, d = x.shape
    n = w.shape[1]
    tm, tn = _pick(m, tm_target), _pick(n, tn_target)
    return pl.pallas_call(
        _norm_matmul_kernel,
        grid=(m // tm, n // tn),
        in_specs=[pl.BlockSpec((tm, d), lambda i, j: (i, 0)),
                  pl.BlockSpec((1, d), lambda i, j: (0, 0)),
                  pl.BlockSpec((d, tn), lambda i, j: (0, j))],
        out_specs=pl.BlockSpec((tm, tn), lambda i, j: (i, j)),
        out_shape=jax.ShapeDtypeStruct((m, n), out_dtype),
        scratch_shapes=[pltpu.VMEM((tm, d), BF16)],
        compiler_params=_cparams("parallel", "arbitrary"),
        name="norm_matmul",
    )(x, g.reshape(1, d), w)


def _da_qkv_kernel(x_ref, g_ref, w_ref, tab_ref, o_ref, h_ref, *, n_head_blocks, q_scale):
    j = pl.program_id(1)

    @pl.when(j == 0)
    def _():
        h_ref[...] = _rms(x_ref[...], g_ref[...]).astype(BF16)

    acc = jnp.dot(h_ref[...], w_ref[...], preferred_element_type=F32)
    rotary = j < 2 * n_head_blocks
    scale = jnp.where(j < n_head_blocks, q_scale, 1.0).astype(F32)
    c = jnp.where(rotary, tab_ref[:, 0:LANES], 1.0) * scale
    s = jnp.where(rotary, tab_ref[:, LANES:2 * LANES], 0.0) * scale
    for ch in range(acc.shape[1] // LANES):
        cols = slice(ch * LANES, (ch + 1) * LANES)
        o_ref[:, cols] = _rope_chunk(acc[:, cols], c, s).astype(o_ref.dtype)


def _da_qkv(x, g, w, tab, tm_target=1024, tn_target=1024):
    m, d = x.shape
    n = w.shape[1]
    tm, tn = _pick(m, tm_target), _pick(d, tn_target)
    q_scale = LOG2E / math.sqrt(DA_HEAD_DIM)
    kern = functools.partial(_da_qkv_kernel, n_head_blocks=d // tn, q_scale=q_scale)
    return pl.pallas_call(
        kern,
        grid=(m // tm, n // tn),
        in_specs=[pl.BlockSpec((tm, d), lambda i, j: (i, 0)),
                  pl.BlockSpec((1, d), lambda i, j: (0, 0)),
                  pl.BlockSpec((d, tn), lambda i, j: (0, j)),
                  pl.BlockSpec((tm, 2 * LANES), lambda i, j: (i, 0))],
        out_specs=pl.BlockSpec((tm, tn), lambda i, j: (i, j)),
        out_shape=jax.ShapeDtypeStruct((m, n), BF16),
        scratch_shapes=[pltpu.VMEM((tm, d), BF16)],
        compiler_params=_cparams("parallel", "arbitrary"),
        name="da_qkv",
    )(x, g.reshape(1, d), w, tab)


N_BUF = 2


def _causal_sweep(qi, tk, ratio, score, softmax, accumulate):
    first = ratio * qi
    below = [slice(j * tk, ratio * tk) for j in range(ratio)]
    score(first, 0)
    softmax(0, 0)
    for j in range(1, ratio):
        score(first + j, j % N_BUF, rows=below[j])
        accumulate(first + j - 1, (j - 1) % N_BUF, rows=below[j - 1])
        softmax(j % N_BUF, 0, rows=below[j])
    accumulate(first + ratio - 1, (ratio - 1) % N_BUF, rows=below[ratio - 1])

    @pl.when(qi > 0)
    def _():
        score(0, 0)
        softmax(0, None)
        score(1, 1)

        def pair(u, carry):
            t = 2 * u
            accumulate(t, 0)
            softmax(1, None)
            score(t + 2, 0)
            accumulate(t + 1, 1)
            softmax(0, None)
            score(t + 3, 1)
            return carry

        lax.fori_loop(0, first // 2 - 1, pair, 0)
        t = first - 2
        accumulate(t, 0)
        softmax(1, None)
        accumulate(t + 1, 1)


def _softmax_update(s, diag, m_prev, l_prev):
    tq, tk = s.shape
    if diag is not None:
        row = lax.broadcasted_iota(jnp.int32, (tq, tk), 0)
        col = lax.broadcasted_iota(jnp.int32, (tq, tk), 1)
        s = jnp.where(col + diag <= row, s, NEG_BIG)
    m_next = jnp.maximum(m_prev, jnp.max(s, axis=1, keepdims=True))
    alpha = jnp.exp2(m_prev - m_next)
    p = jnp.exp2(s - jnp.tile(m_next, (1, tk // LANES)))
    part = p[:, 0:LANES]
    for ch in range(1, tk // LANES):
        part = part + p[:, ch * LANES:(ch + 1) * LANES]
    return m_next, alpha * l_prev + part, alpha, p.astype(BF16)


def _attn_stages(n_streams, tk, q_of, k_of, v_of, s_ref, p_ref, a_ref, m_ref, l_ref, acc_ref):
    nt = (((1,), (1,)), ((), ()))
    every = slice(None)

    def score(blk, par, rows=every):
        keys = pl.ds(pl.multiple_of(blk * tk, tk), tk)
        for c in range(n_streams):
            s_ref[par, c, rows] = lax.dot_general(q_of(c, rows), k_of(c, keys), nt, preferred_element_type=F32)

    def softmax(par, diag, rows=every):
        for c in range(n_streams):
            m_next, l_next, alpha, p = _softmax_update(s_ref[par, c, rows], diag, m_ref[c, rows], l_ref[c, rows])
            m_ref[c, rows] = m_next
            l_ref[c, rows] = l_next
            a_ref[par, c, rows] = alpha
            p_ref[par, c, rows] = p

    def accumulate(blk, par, rows=every):
        v = v_of(pl.ds(pl.multiple_of(blk * tk, tk), tk))
        rep = acc_ref.shape[-1] // LANES
        for c in range(n_streams):
            pv = jnp.dot(p_ref[par, c, rows], v, preferred_element_type=F32)
            acc_ref[c, rows] = acc_ref[c, rows] * jnp.tile(a_ref[par, c, rows], (1, rep)) + pv

    return score, softmax, accumulate


def _attn_scratch(n_streams, tq, tk, vw):
    return [pltpu.VMEM((N_BUF, n_streams, tq, tk), F32),
            pltpu.VMEM((N_BUF, n_streams, tq, tk), BF16),
            pltpu.VMEM((N_BUF, n_streams, tq, LANES), F32),
            pltpu.VMEM((n_streams, tq, LANES), F32),
            pltpu.VMEM((n_streams, tq, LANES), F32),
            pltpu.VMEM((n_streams, tq, vw), F32)]


def _da_attn_kernel(q_ref, k_ref, v_ref, lam_ref, sub_ref, o_ref, s_ref, p_ref, a_ref, m_ref, l_ref, acc_ref,
                    *, tk, ratio, lambda_init):
    qi = pl.program_id(2)
    hd = DA_HEAD_DIM
    m_ref[...] = jnp.full(m_ref.shape, NEG_BIG, F32)
    l_ref[...] = jnp.zeros(l_ref.shape, F32)
    acc_ref[...] = jnp.zeros(acc_ref.shape, F32)

    stages = _attn_stages(
        2, tk,
        lambda c, rows: q_ref[rows, c * hd:(c + 1) * hd],
        lambda c, keys: k_ref[keys, c * hd:(c + 1) * hd],
        lambda rows: v_ref[rows, :],
        s_ref, p_ref, a_ref, m_ref, l_ref, acc_ref)
    _causal_sweep(qi, tk, ratio, *stages)

    lv = lam_ref[...]
    lam = (jnp.exp(jnp.sum(lv[0:1] * lv[1:2], axis=1, keepdims=True))
           - jnp.exp(jnp.sum(lv[2:3] * lv[3:4], axis=1, keepdims=True)) + lambda_init)
    l0 = jnp.sum(l_ref[0], axis=1, keepdims=True)
    l1 = jnp.sum(l_ref[1], axis=1, keepdims=True)
    o = acc_ref[0] / l0 - lam * (acc_ref[1] / l1)
    o_ref[...] = (_rms(o, sub_ref[...]) * (1.0 - lambda_init)).astype(o_ref.dtype)


DA_RATIO = 2
MLA_RATIO = 4


def _attn_blocks(seq, ratio, tk_target=512):
    tk = _pick(seq // ratio, tk_target)
    assert ratio % 2 == 0 and seq % (ratio * tk) == 0 and tk % LANES == 0
    return ratio * tk, tk


def _da_attention(qkv, lam_vecs, subln, lambda_init, batch, seq):
    d3 = qkv.shape[1]
    d = d3 // 3
    hw = 2 * DA_HEAD_DIM
    heads = d // hw
    tq, tk = _attn_blocks(seq, DA_RATIO)
    qkv3 = qkv.reshape(batch, seq, d3)
    kern = functools.partial(_da_attn_kernel, tk=tk, ratio=DA_RATIO, lambda_init=lambda_init)
    resident = pl.Buffered(1)
    out = pl.pallas_call(
        kern,
        grid=(batch, heads, seq // tq),
        in_specs=[pl.BlockSpec((None, tq, hw), lambda b, h, i: (b, i, h)),
                  pl.BlockSpec((None, seq, hw), lambda b, h, i: (b, 0, heads + h), pipeline_mode=resident),
                  pl.BlockSpec((None, seq, hw), lambda b, h, i: (b, 0, 2 * heads + h), pipeline_mode=resident),
                  pl.BlockSpec((4, DA_HEAD_DIM), lambda b, h, i: (0, 0)),
                  pl.BlockSpec((1, hw), lambda b, h, i: (0, 0))],
        out_specs=pl.BlockSpec((None, tq, hw), lambda b, h, i: (b, i, h)),
        out_shape=jax.ShapeDtypeStruct((batch, seq, d), BF16),
        scratch_shapes=_attn_scratch(2, tq, tk, hw),
        compiler_params=_cparams("parallel", "parallel", "arbitrary", vmem=ATTN_VMEM_LIMIT),
        name="da_attn",
    )(qkv3, qkv3, qkv3, lam_vecs, subln.reshape(1, hw))
    return out.reshape(batch * seq, d)


def _mla_proj_kernel(x_ref, g_ref, wd_ref, qn_ref, kvn_ref, wuq_ref, wukv_ref, tab_ref,
                     q_ref, kv_ref, kr_ref, *, q_lora, kv_lora, heads, q_scale):
    h = _rms(x_ref[...], g_ref[...]).astype(BF16)
    d = jnp.dot(h, wd_ref[...], preferred_element_type=F32)
    rc = tab_ref[:, 0:LANES]
    rs = tab_ref[:, LANES:2 * LANES]
    c_q = _rms(d[:, :q_lora], qn_ref[...]).astype(BF16)
    c_kv = _rms(d[:, q_lora:q_lora + kv_lora], kvn_ref[...]).astype(BF16)
    kr = _rope_chunk(d[:, q_lora + kv_lora:q_lora + kv_lora + LANES], rc, rs)
    kr_ref[...] = kr.astype(kr_ref.dtype)
    w = MLA_QK_PAD
    qc, qs = rc * q_scale, rs * q_scale
    for hh in range(heads):
        qh = jnp.dot(c_q, wuq_ref[:, hh * w:(hh + 1) * w], preferred_element_type=F32)
        q_ref[:, hh * w:hh * w + LANES] = (qh[:, :LANES] * q_scale).astype(q_ref.dtype)
        q_ref[:, hh * w + LANES:(hh + 1) * w] = _rope_chunk(qh[:, LANES:], qc, qs).astype(q_ref.dtype)
        kv_ref[:, hh * w:(hh + 1) * w] = jnp.dot(
            c_kv, wukv_ref[:, hh * w:(hh + 1) * w], preferred_element_type=F32).astype(kv_ref.dtype)


def _mla_proj(x, g, wd_pad, qn, kvn, wuq_pad, wukv, tab, heads, tm_target=256):
    m, d = x.shape
    tm = _pick(m, tm_target)
    q_lora, kv_lora = qn.shape[0], kvn.shape[0]
    q_scale = LOG2E / math.sqrt(MLA_NOPE + MLA_ROPE)
    kern = functools.partial(_mla_proj_kernel, q_lora=q_lora, kv_lora=kv_lora, heads=heads, q_scale=q_scale)
    const = lambda i: (0, 0)
    row = lambda i: (i, 0)
    nq, nkv = wuq_pad.shape[1], wukv.shape[1]
    return pl.pallas_call(
        kern,
        grid=(m // tm,),
        in_specs=[pl.BlockSpec((tm, d), row),
                  pl.BlockSpec((1, d), const),
                  pl.BlockSpec(wd_pad.shape, const),
                  pl.BlockSpec((1, q_lora), const),
                  pl.BlockSpec((1, kv_lora), const),
                  pl.BlockSpec(wuq_pad.shape, const),
                  pl.BlockSpec(wukv.shape, const),
                  pl.BlockSpec((tm, 2 * LANES), row)],
        out_specs=[pl.BlockSpec((tm, nq), row),
                   pl.BlockSpec((tm, nkv), row),
                   pl.BlockSpec((tm, LANES), row)],
        out_shape=[jax.ShapeDtypeStruct((m, nq), BF16),
                   jax.ShapeDtypeStruct((m, nkv), BF16),
                   jax.ShapeDtypeStruct((m, LANES), BF16)],
        compiler_params=_cparams("parallel"),
        name="mla_proj",
    )(x, g.reshape(1, d), wd_pad, qn.reshape(1, q_lora), kvn.reshape(1, kv_lora), wuq_pad, wukv, tab)


def _mla_attn_kernel(q_ref, kn_ref, kr_ref, v_ref, o_ref, s_ref, p_ref, a_ref, m_ref, l_ref, acc_ref,
                     *, tk, ratio):
    qi = pl.program_id(2)
    m_ref[...] = jnp.full(m_ref.shape, NEG_BIG, F32)
    l_ref[...] = jnp.zeros(l_ref.shape, F32)
    acc_ref[...] = jnp.zeros(acc_ref.shape, F32)

    stages = _attn_stages(
        1, tk,
        lambda c, rows: q_ref[rows, :],
        lambda c, keys: jnp.concatenate([kn_ref[keys, :], kr_ref[keys, :]], axis=1),
        lambda rows: v_ref[rows, :],
        s_ref, p_ref, a_ref, m_ref, l_ref, acc_ref)
    _causal_sweep(qi, tk, ratio, *stages)
    o_ref[...] = (acc_ref[0] / jnp.sum(l_ref[0], axis=1, keepdims=True)).astype(o_ref.dtype)


def _mla_attention(q, kv, kr, batch, seq, heads):
    tq, tk = _attn_blocks(seq, MLA_RATIO)
    w = MLA_QK_PAD
    q3 = q.reshape(batch, seq, heads * w)
    kv3 = kv.reshape(batch, seq, heads * w)
    kr3 = kr.reshape(batch, seq, LANES)
    kern = functools.partial(_mla_attn_kernel, tk=tk, ratio=MLA_RATIO)
    out = pl.pallas_call(
        kern,
        grid=(batch, heads, seq // tq),
        in_specs=[pl.BlockSpec((None, tq, w), lambda b, h, i: (b, i, h)),
                  pl.BlockSpec((None, seq, LANES), lambda b, h, i: (b, 0, 2 * h)),
                  pl.BlockSpec((None, seq, LANES), lambda b, h, i: (b, 0, 0)),
                  pl.BlockSpec((None, seq, LANES), lambda b, h, i: (b, 0, 2 * h + 1))],
        out_specs=pl.BlockSpec((None, tq, MLA_V), lambda b, h, i: (b, i, h)),
        out_shape=jax.ShapeDtypeStruct((batch, seq, heads * MLA_V), BF16),
        scratch_shapes=_attn_scratch(1, tq, tk, MLA_V),
        compiler_params=_cparams("parallel", "parallel", "arbitrary", vmem=ATTN_VMEM_LIMIT),
        name="mla_attn",
    )(q3, kv3, kr3, kv3)
    return out.reshape(batch * seq, heads * MLA_V)


def _post_attn_kernel(o_ref, wao_ref, x_ref, g_ref, wq_ref, kv_ref, wo_ref, out_ref, *, q_scale):
    x = x_ref[...] + jnp.dot(o_ref[...], wao_ref[...], preferred_element_type=F32)
    h = _rms(x, g_ref[...]).astype(BF16)
    q = (jnp.dot(h, wq_ref[...], preferred_element_type=F32) * q_scale).astype(BF16)
    hd = CA_HEAD_DIM
    kw = CA_HEADS * hd
    outs = []
    for hh in range(CA_HEADS):
        k = kv_ref[:, hh * hd:(hh + 1) * hd]
        v = kv_ref[:, kw + hh * hd:kw + (hh + 1) * hd]
        s = lax.dot_general(q[:, hh * hd:(hh + 1) * hd], k, (((1,), (1,)), ((), ())),
                            preferred_element_type=F32)
        p = jnp.exp2(s - jnp.max(s, axis=1, keepdims=True))
        l = jnp.sum(p, axis=1, keepdims=True)
        outs.append((jnp.dot(p.astype(BF16), v, preferred_element_type=F32) / l).astype(BF16))
    o = jnp.concatenate(outs, axis=1)
    out_ref[...] = x + jnp.dot(o, wo_ref[...], preferred_element_type=F32)


def _post_attn(o, wao, x, g, wq, kv_all, layer, wo, seq, tm_target=512):
    m, d = x.shape
    tm = _pick(seq, tm_target)
    blocks_per_batch = seq // tm
    n_mem = kv_all.shape[1]
    kvw = 2 * CA_HEADS * CA_HEAD_DIM
    kern = functools.partial(_post_attn_kernel, q_scale=LOG2E / math.sqrt(CA_HEAD_DIM))
    const = lambda i: (0, 0)
    once = pl.Buffered(1)
    return pl.pallas_call(
        kern,
        grid=(m // tm,),
        in_specs=[pl.BlockSpec((tm, o.shape[1]), lambda i: (i, 0)),
                  pl.BlockSpec(wao.shape, const, pipeline_mode=once),
                  pl.BlockSpec((tm, d), lambda i: (i, 0)),
                  pl.BlockSpec((1, d), const),
                  pl.BlockSpec(wq.shape, const, pipeline_mode=once),
                  pl.BlockSpec((None, n_mem, kvw), lambda i: (i // blocks_per_batch, 0, layer)),
                  pl.BlockSpec(wo.shape, const, pipeline_mode=once)],
        out_specs=pl.BlockSpec((tm, d), lambda i: (i, 0)),
        out_shape=jax.ShapeDtypeStruct((m, d), F32),
        compiler_params=_cparams("parallel"),
        name="post_attn",
    )(o, wao, x, g.reshape(1, d), wq, kv_all, wo)


def _mlp_kernel(x_ref, g_ref, wup_ref, wdown_ref, fg_ref, out_ref, h_ref, *, final_norm):
    f = pl.program_id(1)

    @pl.when(f == 0)
    def _():
        x = x_ref[...]
        h_ref[...] = _rms(x, g_ref[...]).astype(BF16)
        out_ref[...] = x

    u = jnp.maximum(jnp.dot(h_ref[...], wup_ref[...], preferred_element_type=F32), 0.0)
    out_ref[...] += jnp.dot((u * u).astype(BF16), wdown_ref[...], preferred_element_type=F32)

    if final_norm:
        @pl.when(f == pl.num_programs(1) - 1)
        def _():
            out_ref[...] = _rms(out_ref[...], fg_ref[...])


def _mlp(x, g, wup_all, wdown_all, layer, final_g, final_norm, tm_target=1024, tf_target=512):
    m, d = x.shape
    ff = wup_all.shape[2]
    tm, tf = _pick(m, tm_target), _pick(ff, tf_target)
    kern = functools.partial(_mlp_kernel, final_norm=final_norm)
    return pl.pallas_call(
        kern,
        grid=(m // tm, ff // tf),
        in_specs=[pl.BlockSpec((tm, d), lambda i, f: (i, 0)),
                  pl.BlockSpec((1, d), lambda i, f: (0, 0)),
                  pl.BlockSpec((None, d, tf), lambda i, f: (layer, 0, f)),
                  pl.BlockSpec((None, tf, d), lambda i, f: (layer, f, 0)),
                  pl.BlockSpec((1, d), lambda i, f: (0, 0))],
        out_specs=pl.BlockSpec((tm, d), lambda i, f: (i, 0)),
        out_shape=jax.ShapeDtypeStruct((m, d), F32),
        scratch_shapes=[pltpu.VMEM((tm, d), BF16)],
        compiler_params=_cparams("parallel", "arbitrary"),
        name="mlp",
    )(x, g.reshape(1, d), wup_all, wdown_all, final_g.reshape(1, d))


def kernel(x, mem, positions, attn_norm, cross_norm, mlp_norm, mem_norm, final_norm, da_wqkv, da_lambda, da_subln, da_wo, mla_wdown, mla_q_norm, mla_kv_norm, mla_wuq, mla_wukv, mla_wo, ca_wq, ca_wkv, ca_wo, mlp_wup, mlp_wdown):
    batch, seq, d = x.shape
    depth = attn_norm.shape[0]
    n_mem = mem.shape[1]
    mla_heads = mla_wo.shape[1] // MLA_V

    tab_da = _rope_table(positions, DA_ROT)
    tab_mla = _rope_table(positions, MLA_ROPE)

    wkv_all = jnp.concatenate([ca_wkv[i].astype(BF16) for i in range(depth)], axis=1)
    kv_all = _norm_matmul(mem.reshape(batch * n_mem, d), mem_norm, wkv_all, BF16)
    kv_all = kv_all.reshape(batch, n_mem, wkv_all.shape[1])

    half = MLA_ROPE // 2

    def spread_rope(w):
        gap = jnp.zeros(w.shape[:-1] + (HALF_LANES - half,), w.dtype)
        return jnp.concatenate([w[..., :half], gap, w[..., half:], gap], axis=-1)

    wup_all, wdown_all = mlp_wup.astype(BF16), mlp_wdown.astype(BF16)
    xs = x.reshape(batch * seq, d)
    for i in range(depth):
        j = i // 2
        if i % 2 == 0:
            lambda_init = 0.8 - 0.6 * math.exp(-0.3 * i)
            w = jnp.take(da_wqkv[j], _da_weight_columns(d), axis=1).astype(BF16)
            qkv = _da_qkv(xs, attn_norm[i], w, tab_da)
            o = _da_attention(qkv, da_lambda[j], da_subln[j], lambda_init, batch, seq)
            wao = da_wo[j]
        else:
            q_lora = mla_q_norm.shape[1]
            wd = mla_wdown[j].astype(BF16)
            n_lat = wd.shape[1] - MLA_ROPE
            wd_pad = jnp.concatenate([wd[:, :n_lat], spread_rope(wd[:, n_lat:])], axis=1)
            wuq = mla_wuq[j].astype(BF16).reshape(q_lora, mla_heads, MLA_NOPE + MLA_ROPE)
            wuq_pad = jnp.concatenate([wuq[..., :MLA_NOPE], spread_rope(wuq[..., MLA_NOPE:])], axis=-1)
            wuq_pad = wuq_pad.reshape(q_lora, mla_heads * MLA_QK_PAD)
            q, kv, kr = _mla_proj(xs, attn_norm[i], wd_pad, mla_q_norm[j], mla_kv_norm[j], wuq_pad,
                                  mla_wukv[j].astype(BF16), tab_mla, mla_heads)
            o = _mla_attention(q, kv, kr, batch, seq, mla_heads)
            wao = mla_wo[j]
        xs = _post_attn(o, wao.astype(BF16), xs, cross_norm[i], ca_wq[i].astype(BF16), kv_all, i,
                        ca_wo[i].astype(BF16), seq)
        xs = _mlp(xs, mlp_norm[i], wup_all, wdown_all, i, final_norm, final_norm=(i == depth - 1))
    return xs.reshape(batch, seq, d)
```

```python
import functools
import math

import jax
import jax.numpy as jnp
import numpy as np
from jax import lax
from jax.experimental import pallas as pl
from jax.experimental.pallas import tpu as pltpu

F32 = jnp.float32
BF16 = jnp.bfloat16

EPS = 1e-6
ROPE_THETA = 500000.0
LANES = 128
LOG2E = 1.4426950408889634
NEG_BIG = -1e30

DA_HEAD_DIM = 128
DA_ROT = DA_HEAD_DIM // 4
MLA_NOPE = 128
MLA_ROPE = 64
MLA_V = 128
MLA_QK_PAD = 256
CA_HEADS = 4
CA_HEAD_DIM = 128

VMEM_LIMIT = 56 * 1024 * 1024
ATTN_VMEM_LIMIT = 60 * 1024 * 1024


def _cparams(*sem, vmem=VMEM_LIMIT):
    return pltpu.CompilerParams(dimension_semantics=sem, vmem_limit_bytes=vmem)


def _pick(n, target):
    t = min(n, target)
    while n % t or (t % 8 and t != n):
        t -= 1
    return t


def _rms(xf, g):
    ms = jnp.mean(xf * xf, axis=-1, keepdims=True)
    return xf * lax.rsqrt(ms + EPS) * g


HALF_LANES = LANES // 2


def _rope_chunk(xc, c, s):
    return xc * c + pltpu.roll(xc, HALF_LANES, 1) * s


def _rope_lane_perm(rot):
    half = rot // 2
    perm = list(range(LANES))
    perm[half:rot], perm[HALF_LANES:HALF_LANES + half] = perm[HALF_LANES:HALF_LANES + half], perm[half:rot]
    return perm


def _da_weight_columns(d):
    perm = _rope_lane_perm(DA_ROT)
    cols = []
    for chunk in range(3 * d // LANES):
        lanes = perm if chunk < 2 * d // LANES else range(LANES)
        cols.extend(chunk * LANES + lane for lane in lanes)
    return np.asarray(cols, np.int32)


def _rope_table(positions, rot):
    half = rot // 2
    inv_freq = ROPE_THETA ** (-jnp.arange(0, rot, 2, dtype=F32) / rot)
    ang = positions.astype(F32).reshape(-1)[:, None] * inv_freq
    cos, sin = jnp.cos(ang), jnp.sin(ang)
    n = ang.shape[0]
    src = jnp.concatenate([cos, sin, -sin, jnp.ones((n, 1), F32), jnp.zeros((n, 1), F32)], axis=1)
    one, zero = 3 * half, 3 * half + 1
    c_idx = [j if j < half else one for j in range(HALF_LANES)] * 2
    s_idx = ([2 * half + j if j < half else zero for j in range(HALF_LANES)]
             + [half + j if j < half else zero for j in range(HALF_LANES)])
    return jnp.take(src, np.asarray(c_idx + s_idx, np.int32), axis=1)


def _norm_matmul_kernel(x_ref, g_ref, w_ref, o_ref, h_ref):
    @pl.when(pl.program_id(1) == 0)
    def _():
        h_ref[...] = _rms(x_ref[...], g_ref[...]).astype(BF16)

    o_ref[...] = jnp.dot(h_ref[...], w_ref[...], preferred_element_type=F32).astype(o_ref.dtype)


def _norm_matmul(x, g, w, out_dtype, tm_target=512, tn_target=1024):
    m, d = x.shape
    n = w.shape[1]
    tm, tn = _pick(m, tm_target), _pick(n, tn_target)
    return pl.pallas_call(
        _norm_matmul_kernel,
        grid=(m // tm, n // tn),
        in_specs=[pl.BlockSpec((tm, d), lambda i, j: (i, 0)),
                  pl.BlockSpec((1, d), lambda i, j: (0, 0)),
                  pl.BlockSpec((d, tn), lambda i, j: (0, j))],
        out_specs=pl.BlockSpec((tm, tn), lambda i, j: (i, j)),
        out_shape=jax.ShapeDtypeStruct((m, n), out_dtype),
        scratch_shapes=[pltpu.VMEM((tm, d), BF16)],
        compiler_params=_cparams("parallel", "arbitrary"),
        name="norm_matmul",
    )(x, g.reshape(1, d), w)


def _da_qkv_kernel(x_ref, g_ref, w_ref, tab_ref, o_ref, h_ref, *, n_head_blocks, q_scale):
    j = pl.program_id(1)

    @pl.when(j == 0)
    def _():
        h_ref[...] = _rms(x_ref[...], g_ref[...]).astype(BF16)

    acc = jnp.dot(h_ref[...], w_ref[...], preferred_element_type=F32)
    rotary = j < 2 * n_head_blocks
    scale = jnp.where(j < n_head_blocks, q_scale, 1.0).astype(F32)
    c = jnp.where(rotary, tab_ref[:, 0:LANES], 1.0) * scale
    s = jnp.where(rotary, tab_ref[:, LANES:2 * LANES], 0.0) * scale
    for ch in range(acc.shape[1] // LANES):
        cols = slice(ch * LANES, (ch + 1) * LANES)
        o_ref[:, cols] = _rope_chunk(acc[:, cols], c, s).astype(o_ref.dtype)


def _da_qkv(x, g, w, tab, tm_target=1024, tn_target=1024):
    m, d = x.shape
    n = w.shape[1]
    tm, tn = _pick(m, tm_target), _pick(d, tn_target)
    q_scale = LOG2E / math.sqrt(DA_HEAD_DIM)
    kern = functools.partial(_da_qkv_kernel, n_head_blocks=d // tn, q_scale=q_scale)
    return pl.pallas_call(
        kern,
        grid=(m // tm, n // tn),
        in_specs=[pl.BlockSpec((tm, d), lambda i, j: (i, 0)),
                  pl.BlockSpec((1, d), lambda i, j: (0, 0)),
                  pl.BlockSpec((d, tn), lambda i, j: (0, j)),
                  pl.BlockSpec((tm, 2 * LANES), lambda i, j: (i, 0))],
        out_specs=pl.BlockSpec((tm, tn), lambda i, j: (i, j)),
        out_shape=jax.ShapeDtypeStruct((m, n), BF16),
        scratch_shapes=[pltpu.VMEM((tm, d), BF16)],
        compiler_params=_cparams("parallel", "arbitrary"),
        name="da_qkv",
    )(x, g.reshape(1, d), w, tab)


N_BUF = 2


def _causal_sweep(qi, tk, ratio, score, softmax, accumulate):
    first = ratio * qi
    below = [slice(j * tk, ratio * tk) for j in range(ratio)]
    score(first, 0)
    softmax(0, 0, first=True)
    for j in range(1, ratio):
        score(first + j, j % N_BUF, rows=below[j])
        accumulate(first + j - 1, (j - 1) % N_BUF, rows=below[j - 1], first=j == 1)
        softmax(j % N_BUF, 0, rows=below[j])
    accumulate(first + ratio - 1, (ratio - 1) % N_BUF, rows=below[ratio - 1])

    @pl.when(qi > 0)
    def _():
        score(0, 0)
        softmax(0, None)
        score(1, 1)

        def pair(u, carry):
            t = 2 * u
            accumulate(t, 0)
            softmax(1, None)
            score(t + 2, 0)
            accumulate(t + 1, 1)
            softmax(0, None)
            score(t + 3, 1)
            return carry

        lax.fori_loop(0, first // 2 - 1, pair, 0)
        t = first - 2
        accumulate(t, 0)
        softmax(1, None)
        accumulate(t + 1, 1)


def _softmax_update(s, diag, m_prev, l_prev):
    tq, tk = s.shape
    if diag is not None:
        row = lax.broadcasted_iota(jnp.int32, (tq, tk), 0)
        col = lax.broadcasted_iota(jnp.int32, (tq, tk), 1)
        s = jnp.where(col + diag <= row, s, NEG_BIG)
    m_blk = jnp.max(s, axis=1, keepdims=True)
    first = m_prev is None
    m_next = jnp.broadcast_to(m_blk, (tq, LANES)) if first else jnp.maximum(m_prev, m_blk)
    p = jnp.exp2(s - jnp.tile(m_next, (1, tk // LANES)))
    part = p[:, 0:LANES]
    for ch in range(1, tk // LANES):
        part = part + p[:, ch * LANES:(ch + 1) * LANES]
    if first:
        return m_next, part, None, p.astype(BF16)
    alpha = jnp.exp2(m_prev - m_next)
    return m_next, alpha * l_prev + part, alpha, p.astype(BF16)


def _attn_stages(n_streams, tk, q_of, k_of, v_of, s_ref, p_ref, a_ref, m_ref, l_ref, acc_ref):
    nt = (((1,), (1,)), ((), ()))
    every = slice(None)

    def score(blk, par, rows=every):
        keys = pl.ds(pl.multiple_of(blk * tk, tk), tk)
        for c in range(n_streams):
            s_ref[par, c, rows] = lax.dot_general(q_of(c, rows), k_of(c, keys), nt, preferred_element_type=F32)

    def softmax(par, diag, rows=every, first=False):
        for c in range(n_streams):
            m_prev, l_prev = (None, None) if first else (m_ref[c, rows], l_ref[c, rows])
            m_next, l_next, alpha, p = _softmax_update(s_ref[par, c, rows], diag, m_prev, l_prev)
            m_ref[c, rows] = m_next
            l_ref[c, rows] = l_next
            if not first:
                a_ref[par, c, rows] = alpha
            p_ref[par, c, rows] = p

    def accumulate(blk, par, rows=every, first=False):
        v = v_of(pl.ds(pl.multiple_of(blk * tk, tk), tk))
        rep = acc_ref.shape[-1] // LANES
        for c in range(n_streams):
            pv = jnp.dot(p_ref[par, c, rows], v, preferred_element_type=F32)
            if first:
                acc_ref[c, rows] = pv
            else:
                acc_ref[c, rows] = acc_ref[c, rows] * jnp.tile(a_ref[par, c, rows], (1, rep)) + pv

    return score, softmax, accumulate


def _attn_scratch(n_streams, tq, tk, vw):
    return [pltpu.VMEM((N_BUF, n_streams, tq, tk), F32),
            pltpu.VMEM((N_BUF, n_streams, tq, tk), BF16),
            pltpu.VMEM((N_BUF, n_streams, tq, LANES), F32),
            pltpu.VMEM((n_streams, tq, LANES), F32),
            pltpu.VMEM((n_streams, tq, LANES), F32),
            pltpu.VMEM((n_streams, tq, vw), F32)]


def _da_attn_kernel(q_ref, k_ref, v_ref, lam_ref, sub_ref, o_ref, s_ref, p_ref, a_ref, m_ref, l_ref, acc_ref,
                    *, tk, ratio, lambda_init):
    qi = pl.program_id(2)
    hd = DA_HEAD_DIM

    stages = _attn_stages(
        2, tk,
        lambda c, rows: q_ref[rows, c * hd:(c + 1) * hd],
        lambda c, keys: k_ref[keys, c * hd:(c + 1) * hd],
        lambda rows: v_ref[rows, :],
        s_ref, p_ref, a_ref, m_ref, l_ref, acc_ref)
    _causal_sweep(qi, tk, ratio, *stages)

    lv = lam_ref[...]
    lam = (jnp.exp(jnp.sum(lv[0:1] * lv[1:2], axis=1, keepdims=True))
           - jnp.exp(jnp.sum(lv[2:3] * lv[3:4], axis=1, keepdims=True)) + lambda_init)
    l0 = jnp.sum(l_ref[0], axis=1, keepdims=True)
    l1 = jnp.sum(l_ref[1], axis=1, keepdims=True)
    o = acc_ref[0] / l0 - lam * (acc_ref[1] / l1)
    o_ref[...] = (_rms(o, sub_ref[...]) * (1.0 - lambda_init)).astype(o_ref.dtype)


DA_RATIO = 2
MLA_RATIO = 4


def _attn_blocks(seq, ratio, tk_target=512):
    tk = _pick(seq // ratio, tk_target)
    assert ratio % 2 == 0 and seq % (ratio * tk) == 0 and tk % LANES == 0
    return ratio * tk, tk


def _da_attention(qkv, lam_vecs, subln, lambda_init, batch, seq):
    d3 = qkv.shape[1]
    d = d3 // 3
    hw = 2 * DA_HEAD_DIM
    heads = d // hw
    tq, tk = _attn_blocks(seq, DA_RATIO)
    qkv3 = qkv.reshape(batch, seq, d3)
    kern = functools.partial(_da_attn_kernel, tk=tk, ratio=DA_RATIO, lambda_init=lambda_init)
    resident = pl.Buffered(1)
    out = pl.pallas_call(
        kern,
        grid=(batch, heads, seq // tq),
        in_specs=[pl.BlockSpec((None, tq, hw), lambda b, h, i: (b, i, h)),
                  pl.BlockSpec((None, seq, hw), lambda b, h, i: (b, 0, heads + h), pipeline_mode=resident),
                  pl.BlockSpec((None, seq, hw), lambda b, h, i: (b, 0, 2 * heads + h), pipeline_mode=resident),
                  pl.BlockSpec((4, DA_HEAD_DIM), lambda b, h, i: (0, 0)),
                  pl.BlockSpec((1, hw), lambda b, h, i: (0, 0))],
        out_specs=pl.BlockSpec((None, tq, hw), lambda b, h, i: (b, i, h)),
        out_shape=jax.ShapeDtypeStruct((batch, seq, d), BF16),
        scratch_shapes=_attn_scratch(2, tq, tk, hw),
        compiler_params=_cparams("parallel", "parallel", "arbitrary", vmem=ATTN_VMEM_LIMIT),
        name="da_attn",
    )(qkv3, qkv3, qkv3, lam_vecs, subln.reshape(1, hw))
    return out.reshape(batch * seq, d)


def _mla_proj_kernel(x_ref, g_ref, wd_ref, qn_ref, kvn_ref, wuq_ref, wukv_ref, tab_ref,
                     q_ref, kv_ref, kr_ref, *, q_lora, kv_lora, heads, q_scale):
    h = _rms(x_ref[...], g_ref[...]).astype(BF16)
    d = jnp.dot(h, wd_ref[...], preferred_element_type=F32)
    rc = tab_ref[:, 0:LANES]
    rs = tab_ref[:, LANES:2 * LANES]
    c_q = _rms(d[:, :q_lora], qn_ref[...]).astype(BF16)
    c_kv = _rms(d[:, q_lora:q_lora + kv_lora], kvn_ref[...]).astype(BF16)
    kr = _rope_chunk(d[:, q_lora + kv_lora:q_lora + kv_lora + LANES], rc, rs)
    kr_ref[...] = kr.astype(kr_ref.dtype)
    w = MLA_QK_PAD
    qc, qs = rc * q_scale, rs * q_scale
    for hh in range(heads):
        qh = jnp.dot(c_q, wuq_ref[:, hh * w:(hh + 1) * w], preferred_element_type=F32)
        q_ref[:, hh * w:hh * w + LANES] = (qh[:, :LANES] * q_scale).astype(q_ref.dtype)
        q_ref[:, hh * w + LANES:(hh + 1) * w] = _rope_chunk(qh[:, LANES:], qc, qs).astype(q_ref.dtype)
        kv_ref[:, hh * w:(hh + 1) * w] = jnp.dot(
            c_kv, wukv_ref[:, hh * w:(hh + 1) * w], preferred_element_type=F32).astype(kv_ref.dtype)


def _mla_proj(x, g, wd_pad, qn, kvn, wuq_pad, wukv, tab, heads, tm_target=256):
    m, d = x.shape
    tm = _pick(m, tm_target)
    q_lora, kv_lora = qn.shape[0], kvn.shape[0]
    q_scale = LOG2E / math.sqrt(MLA_NOPE + MLA_ROPE)
    kern = functools.partial(_mla_proj_kernel, q_lora=q_lora, kv_lora=kv_lora, heads=heads, q_scale=q_scale)
    const = lambda i: (0, 0)
    row = lambda i: (i, 0)
    nq, nkv = wuq_pad.shape[1], wukv.shape[1]
    return pl.pallas_call(
        kern,
        grid=(m // tm,),
        in_specs=[pl.BlockSpec((tm, d), row),
                  pl.BlockSpec((1, d), const),
                  pl.BlockSpec(wd_pad.shape, const),
                  pl.BlockSpec((1, q_lora), const),
                  pl.BlockSpec((1, kv_lora), const),
                  pl.BlockSpec(wuq_pad.shape, const),
                  pl.BlockSpec(wukv.shape, const),
                  pl.BlockSpec((tm, 2 * LANES), row)],
        out_specs=[pl.BlockSpec((tm, nq), row),
                   pl.BlockSpec((tm, nkv), row),
                   pl.BlockSpec((tm, LANES), row)],
        out_shape=[jax.ShapeDtypeStruct((m, nq), BF16),
                   jax.ShapeDtypeStruct((m, nkv), BF16),
                   jax.ShapeDtypeStruct((m, LANES), BF16)],
        compiler_params=_cparams("parallel"),
        name="mla_proj",
    )(x, g.reshape(1, d), wd_pad, qn.reshape(1, q_lora), kvn.reshape(1, kv_lora), wuq_pad, wukv, tab)


def _mla_attn_kernel(q_ref, kn_ref, kr_ref, v_ref, o_ref, s_ref, p_ref, a_ref, m_ref, l_ref, acc_ref,
                     *, tk, ratio):
    qi = pl.program_id(2)

    stages = _attn_stages(
        1, tk,
        lambda c, rows: q_ref[rows, :],
        lambda c, keys: jnp.concatenate([kn_ref[keys, :], kr_ref[keys, :]], axis=1),
        lambda rows: v_ref[rows, :],
        s_ref, p_ref, a_ref, m_ref, l_ref, acc_ref)
    _causal_sweep(qi, tk, ratio, *stages)
    o_ref[...] = (acc_ref[0] / jnp.sum(l_ref[0], axis=1, keepdims=True)).astype(o_ref.dtype)


def _mla_attention(q, kv, kr, batch, seq, heads):
    tq, tk = _attn_blocks(seq, MLA_RATIO)
    w = MLA_QK_PAD
    q3 = q.reshape(batch, seq, heads * w)
    kv3 = kv.reshape(batch, seq, heads * w)
    kr3 = kr.reshape(batch, seq, LANES)
    kern = functools.partial(_mla_attn_kernel, tk=tk, ratio=MLA_RATIO)
    out = pl.pallas_call(
        kern,
        grid=(batch, heads, seq // tq),
        in_specs=[pl.BlockSpec((None, tq, w), lambda b, h, i: (b, i, h)),
                  pl.BlockSpec((None, seq, LANES), lambda b, h, i: (b, 0, 2 * h)),
                  pl.BlockSpec((None, seq, LANES), lambda b, h, i: (b, 0, 0)),
                  pl.BlockSpec((None, seq, LANES), lambda b, h, i: (b, 0, 2 * h + 1))],
        out_specs=pl.BlockSpec((None, tq, MLA_V), lambda b, h, i: (b, i, h)),
        out_shape=jax.ShapeDtypeStruct((batch, seq, heads * MLA_V), BF16),
        scratch_shapes=_attn_scratch(1, tq, tk, MLA_V),
        compiler_params=_cparams("parallel", "parallel", "arbitrary", vmem=ATTN_VMEM_LIMIT),
        name="mla_attn",
    )(q3, kv3, kr3, kv3)
    return out.reshape(batch * seq, heads * MLA_V)


def _post_attn_kernel(o_ref, wao_ref, x_ref, g_ref, wq_ref, kv_ref, wo_ref, out_ref, *, q_scale):
    x = x_ref[...] + jnp.dot(o_ref[...], wao_ref[...], preferred_element_type=F32)
    h = _rms(x, g_ref[...]).astype(BF16)
    q = (jnp.dot(h, wq_ref[...], preferred_element_type=F32) * q_scale).astype(BF16)
    hd = CA_HEAD_DIM
    kw = CA_HEADS * hd
    outs = []
    for hh in range(CA_HEADS):
        k = kv_ref[:, hh * hd:(hh + 1) * hd]
        v = kv_ref[:, kw + hh * hd:kw + (hh + 1) * hd]
        s = lax.dot_general(q[:, hh * hd:(hh + 1) * hd], k, (((1,), (1,)), ((), ())),
                            preferred_element_type=F32)
        p = jnp.exp2(s - jnp.max(s, axis=1, keepdims=True))
        l = jnp.sum(p, axis=1, keepdims=True)
        outs.append((jnp.dot(p.astype(BF16), v, preferred_element_type=F32) / l).astype(BF16))
    o = jnp.concatenate(outs, axis=1)
    out_ref[...] = x + jnp.dot(o, wo_ref[...], preferred_element_type=F32)


def _post_attn(o, wao, x, g, wq, kv_all, layer, wo, seq, tm_target=512):
    m, d = x.shape
    tm = _pick(seq, tm_target)
    blocks_per_batch = seq // tm
    n_mem = kv_all.shape[1]
    kvw = 2 * CA_HEADS * CA_HEAD_DIM
    kern = functools.partial(_post_attn_kernel, q_scale=LOG2E / math.sqrt(CA_HEAD_DIM))
    const = lambda i: (0, 0)
    once = pl.Buffered(1)
    return pl.pallas_call(
        kern,
        grid=(m // tm,),
        in_specs=[pl.BlockSpec((tm, o.shape[1]), lambda i: (i, 0)),
                  pl.BlockSpec(wao.shape, const, pipeline_mode=once),
                  pl.BlockSpec((tm, d), lambda i: (i, 0)),
                  pl.BlockSpec((1, d), const),
                  pl.BlockSpec(wq.shape, const, pipeline_mode=once),
                  pl.BlockSpec((None, n_mem, kvw), lambda i: (i // blocks_per_batch, 0, layer)),
                  pl.BlockSpec(wo.shape, const, pipeline_mode=once)],
        out_specs=pl.BlockSpec((tm, d), lambda i: (i, 0)),
        out_shape=jax.ShapeDtypeStruct((m, d), F32),
        compiler_params=_cparams("parallel"),
        name="post_attn",
    )(o, wao, x, g.reshape(1, d), wq, kv_all, wo)


def _mlp_kernel(x_ref, g_ref, wup_ref, wdown_ref, fg_ref, out_ref, h_ref, *, final_norm):
    f = pl.program_id(1)

    @pl.when(f == 0)
    def _():
        x = x_ref[...]
        h_ref[...] = _rms(x, g_ref[...]).astype(BF16)
        out_ref[...] = x

    u = jnp.maximum(jnp.dot(h_ref[...], wup_ref[...], preferred_element_type=F32), 0.0)
    out_ref[...] += jnp.dot((u * u).astype(BF16), wdown_ref[...], preferred_element_type=F32)

    if final_norm:
        @pl.when(f == pl.num_programs(1) - 1)
        def _():
            out_ref[...] = _rms(out_ref[...], fg_ref[...])


def _mlp(x, g, wup_all, wdown_all, layer, final_g, final_norm, tm_target=1024, tf_target=512):
    m, d = x.shape
    ff = wup_all.shape[2]
    tm, tf = _pick(m, tm_target), _pick(ff, tf_target)
    kern = functools.partial(_mlp_kernel, final_norm=final_norm)
    return pl.pallas_call(
        kern,
        grid=(m // tm, ff // tf),
        in_specs=[pl.BlockSpec((tm, d), lambda i, f: (i, 0)),
                  pl.BlockSpec((1, d), lambda i, f: (0, 0)),
                  pl.BlockSpec((None, d, tf), lambda i, f: (layer, 0, f)),
                  pl.BlockSpec((None, tf, d), lambda i, f: (layer, f, 0)),
                  pl.BlockSpec((1, d), lambda i, f: (0, 0))],
        out_specs=pl.BlockSpec((tm, d), lambda i, f: (i, 0)),
        out_shape=jax.ShapeDtypeStruct((m, d), F32),
        scratch_shapes=[pltpu.VMEM((tm, d), BF16)],
        compiler_params=_cparams("parallel", "arbitrary"),
        name="mlp",
    )(x, g.reshape(1, d), wup_all, wdown_all, final_g.reshape(1, d))


def kernel(x, mem, positions, attn_norm, cross_norm, mlp_norm, mem_norm, final_norm, da_wqkv, da_lambda, da_subln, da_wo, mla_wdown, mla_q_norm, mla_kv_norm, mla_wuq, mla_wukv, mla_wo, ca_wq, ca_wkv, ca_wo, mlp_wup, mlp_wdown):
    batch, seq, d = x.shape
    depth = attn_norm.shape[0]
    n_mem = mem.shape[1]
    mla_heads = mla_wo.shape[1] // MLA_V

    tab_da = _rope_table(positions, DA_ROT)
    tab_mla = _rope_table(positions, MLA_ROPE)

    wkv_all = jnp.concatenate([ca_wkv[i].astype(BF16) for i in range(depth)], axis=1)
    kv_all = _norm_matmul(mem.reshape(batch * n_mem, d), mem_norm, wkv_all, BF16)
    kv_all = kv_all.reshape(batch, n_mem, wkv_all.shape[1])

    half = MLA_ROPE // 2

    def spread_rope(w):
        gap = jnp.zeros(w.shape[:-1] + (HALF_LANES - half,), w.dtype)
        return jnp.concatenate([w[..., :half], gap, w[..., half:], gap], axis=-1)

    wup_all, wdown_all = mlp_wup.astype(BF16), mlp_wdown.astype(BF16)
    xs = x.reshape(batch * seq, d)
    for i in range(depth):
        j = i // 2
        if i % 2 == 0:
            lambda_init = 0.8 - 0.6 * math.exp(-0.3 * i)
            w = jnp.take(da_wqkv[j], _da_weight_columns(d), axis=1).astype(BF16)
            qkv = _da_qkv(xs, attn_norm[i], w, tab_da)
            o = _da_attention(qkv, da_lambda[j], da_subln[j], lambda_init, batch, seq)
            wao = da_wo[j]
        else:
            q_lora = mla_q_norm.shape[1]
            wd = mla_wdown[j].astype(BF16)
            n_lat = wd.shape[1] - MLA_ROPE
            wd_pad = jnp.concatenate([wd[:, :n_lat], spread_rope(wd[:, n_lat:])], axis=1)
            wuq = mla_wuq[j].astype(BF16).reshape(q_lora, mla_heads, MLA_NOPE + MLA_ROPE)
            wuq_pad = jnp.concatenate([wuq[..., :MLA_NOPE], spread_rope(wuq[..., MLA_NOPE:])], axis=-1)
            wuq_pad = wuq_pad.reshape(q_lora, mla_heads * MLA_QK_PAD)
            q, kv, kr = _mla_proj(xs, attn_norm[i], wd_pad, mla_q_norm[j], mla_kv_norm[j], wuq_pad,
                                  mla_wukv[j].astype(BF16), tab_mla, mla_heads)
            o = _mla_attention(q, kv, kr, batch, seq, mla_heads)
            wao = mla_wo[j]
        xs = _post_attn(o, wao.astype(BF16), xs, cross_norm[i], ca_wq[i].astype(BF16), kv_all, i,
                        ca_wo[i].astype(BF16), seq)
        xs = _mlp(xs, mlp_norm[i], wup_all, wdown_all, i, final_norm, final_norm=(i == depth - 1))
    return xs.reshape(batch, seq, d)
```

```python
import functools
import math

import jax
import jax.numpy as jnp
import numpy as np
from jax import lax
from jax.experimental import pallas as pl
from jax.experimental.pallas import tpu as pltpu

F32 = jnp.float32
BF16 = jnp.bfloat16

EPS = 1e-6
ROPE_THETA = 500000.0
LANES = 128
LOG2E = 1.4426950408889634
NEG_BIG = -1e30

DA_HEAD_DIM = 128
DA_ROT = DA_HEAD_DIM // 4
MLA_NOPE = 128
MLA_ROPE = 64
MLA_V = 128
MLA_QK_PAD = 256
CA_HEADS = 4
CA_HEAD_DIM = 128

VMEM_LIMIT = 56 * 1024 * 1024
ATTN_VMEM_LIMIT = 60 * 1024 * 1024


def _cparams(*sem, vmem=VMEM_LIMIT):
    return pltpu.CompilerParams(dimension_semantics=sem, vmem_limit_bytes=vmem)


def _pick(n, target):
    t = min(n, target)
    while n % t or (t % 8 and t != n):
        t -= 1
    return t


def _rms(xf, g):
    ms = jnp.mean(xf * xf, axis=-1, keepdims=True)
    return xf * lax.rsqrt(ms + EPS) * g


HALF_LANES = LANES // 2


def _rope_chunk(xc, c, s):
    return xc * c + pltpu.roll(xc, HALF_LANES, 1) * s


def _rope_lane_perm(rot):
    half = rot // 2
    perm = list(range(LANES))
    perm[half:rot], perm[HALF_LANES:HALF_LANES + half] = perm[HALF_LANES:HALF_LANES + half], perm[half:rot]
    return perm


def _da_weight_columns(d):
    perm = _rope_lane_perm(DA_ROT)
    cols = []
    for chunk in range(3 * d // LANES):
        lanes = perm if chunk < 2 * d // LANES else range(LANES)
        cols.extend(chunk * LANES + lane for lane in lanes)
    return np.asarray(cols, np.int32)


def _rope_table(positions, rot):
    half = rot // 2
    inv_freq = ROPE_THETA ** (-jnp.arange(0, rot, 2, dtype=F32) / rot)
    ang = positions.astype(F32).reshape(-1)[:, None] * inv_freq
    cos, sin = jnp.cos(ang), jnp.sin(ang)
    n = ang.shape[0]
    src = jnp.concatenate([cos, sin, -sin, jnp.ones((n, 1), F32), jnp.zeros((n, 1), F32)], axis=1)
    one, zero = 3 * half, 3 * half + 1
    c_idx = [j if j < half else one for j in range(HALF_LANES)] * 2
    s_idx = ([2 * half + j if j < half else zero for j in range(HALF_LANES)]
             + [half + j if j < half else zero for j in range(HALF_LANES)])
    return jnp.take(src, np.asarray(c_idx + s_idx, np.int32), axis=1)


def _norm_matmul_kernel(x_ref, g_ref, w_ref, o_ref, h_ref):
    @pl.when(pl.program_id(1) == 0)
    def _():
        h_ref[...] = _rms(x_ref[...], g_ref[...]).astype(BF16)

    o_ref[...] = jnp.dot(h_ref[...], w_ref[...], preferred_element_type=F32).astype(o_ref.dtype)


def _norm_matmul(x, g, w, out_dtype, tm_target=512, tn_target=1024):
    m, d = x.shape
    n = w.shape[1]
    tm, tn = _pick(m, tm_target), _pick(n, tn_target)
    return pl.pallas_call(
        _norm_matmul_kernel,
        grid=(m // tm, n // tn),
        in_specs=[pl.BlockSpec((tm, d), lambda i, j: (i, 0)),
                  pl.BlockSpec((1, d), lambda i, j: (0, 0)),
                  pl.BlockSpec((d, tn), lambda i, j: (0, j))],
        out_specs=pl.BlockSpec((tm, tn), lambda i, j: (i, j)),
        out_shape=jax.ShapeDtypeStruct((m, n), out_dtype),
        scratch_shapes=[pltpu.VMEM((tm, d), BF16)],
        compiler_params=_cparams("parallel", "arbitrary"),
        name="norm_matmul",
    )(x, g.reshape(1, d), w)


def _da_qkv_kernel(x_ref, g_ref, w_ref, tab_ref, o_ref, h_ref, *, n_head_blocks, q_scale):
    j = pl.program_id(1)

    @pl.when(j == 0)
    def _():
        h_ref[...] = _rms(x_ref[...], g_ref[...]).astype(BF16)

    acc = jnp.dot(h_ref[...], w_ref[...], preferred_element_type=F32)
    rotary = j < 2 * n_head_blocks
    scale = jnp.where(j < n_head_blocks, q_scale, 1.0).astype(F32)
    c = jnp.where(rotary, tab_ref[:, 0:LANES], 1.0) * scale
    s = jnp.where(rotary, tab_ref[:, LANES:2 * LANES], 0.0) * scale
    for ch in range(acc.shape[1] // LANES):
        cols = slice(ch * LANES, (ch + 1) * LANES)
        o_ref[:, cols] = _rope_chunk(acc[:, cols], c, s).astype(o_ref.dtype)


def _da_qkv(x, g, w, tab, tm_target=1024, tn_target=1024):
    m, d = x.shape
    n = w.shape[1]
    tm, tn = _pick(m, tm_target), _pick(d, tn_target)
    q_scale = LOG2E / math.sqrt(DA_HEAD_DIM)
    kern = functools.partial(_da_qkv_kernel, n_head_blocks=d // tn, q_scale=q_scale)
    return pl.pallas_call(
        kern,
        grid=(m // tm, n // tn),
        in_specs=[pl.BlockSpec((tm, d), lambda i, j: (i, 0)),
                  pl.BlockSpec((1, d), lambda i, j: (0, 0)),
                  pl.BlockSpec((d, tn), lambda i, j: (0, j)),
                  pl.BlockSpec((tm, 2 * LANES), lambda i, j: (i, 0))],
        out_specs=pl.BlockSpec((tm, tn), lambda i, j: (i, j)),
        out_shape=jax.ShapeDtypeStruct((m, n), BF16),
        scratch_shapes=[pltpu.VMEM((tm, d), BF16)],
        compiler_params=_cparams("parallel", "arbitrary"),
        name="da_qkv",
    )(x, g.reshape(1, d), w, tab)


N_BUF = 2


def _causal_sweep(qi, tk, ratio, score, softmax, accumulate):
    first = ratio * qi
    below = [slice(j * tk, ratio * tk) for j in range(ratio)]

    def diagonal():
        score(first, 0)
        softmax(0, 0, first=True)
        for j in range(1, ratio):
            score(first + j, j % N_BUF, rows=below[j])
            accumulate(first + j - 1, (j - 1) % N_BUF, rows=below[j - 1], first=j == 1)
            softmax(j % N_BUF, 0, rows=below[j])
        accumulate(first + ratio - 1, (ratio - 1) % N_BUF, rows=below[ratio - 1])

    pl.when(qi == 0)(diagonal)

    @pl.when(qi > 0)
    def _():
        diagonal()
        score(0, 0)
        softmax(0, None)
        score(1, 1)

        def pair(u, carry):
            t = 2 * u
            accumulate(t, 0)
            softmax(1, None)
            score(t + 2, 0)
            accumulate(t + 1, 1)
            softmax(0, None)
            score(t + 3, 1)
            return carry

        lax.fori_loop(0, first // 2 - 1, pair, 0)
        t = first - 2
        accumulate(t, 0)
        softmax(1, None)
        accumulate(t + 1, 1)


def _softmax_update(s, diag, m_prev, l_prev):
    tq, tk = s.shape
    if diag is not None:
        row = lax.broadcasted_iota(jnp.int32, (tq, tk), 0)
        col = lax.broadcasted_iota(jnp.int32, (tq, tk), 1)
        s = jnp.where(col + diag <= row, s, NEG_BIG)
    m_blk = jnp.max(s, axis=1, keepdims=True)
    first = m_prev is None
    m_next = jnp.broadcast_to(m_blk, (tq, LANES)) if first else jnp.maximum(m_prev, m_blk)
    p = jnp.exp2(s - jnp.tile(m_next, (1, tk // LANES)))
    part = p[:, 0:LANES]
    for ch in range(1, tk // LANES):
        part = part + p[:, ch * LANES:(ch + 1) * LANES]
    if first:
        return m_next, part, None, p.astype(BF16)
    alpha = jnp.exp2(m_prev - m_next)
    return m_next, alpha * l_prev + part, alpha, p.astype(BF16)


def _attn_stages(n_streams, tk, q_of, k_of, v_of, s_ref, p_ref, a_ref, m_ref, l_ref, acc_ref):
    nt = (((1,), (1,)), ((), ()))
    every = slice(None)

    def score(blk, par, rows=every):
        keys = pl.ds(pl.multiple_of(blk * tk, tk), tk)
        for c in range(n_streams):
            s_ref[par, c, rows] = lax.dot_general(q_of(c, rows), k_of(c, keys), nt, preferred_element_type=F32)

    def softmax(par, diag, rows=every, first=False):
        for c in range(n_streams):
            m_prev, l_prev = (None, None) if first else (m_ref[c, rows], l_ref[c, rows])
            m_next, l_next, alpha, p = _softmax_update(s_ref[par, c, rows], diag, m_prev, l_prev)
            m_ref[c, rows] = m_next
            l_ref[c, rows] = l_next
            if not first:
                a_ref[par, c, rows] = alpha
            p_ref[par, c, rows] = p

    def accumulate(blk, par, rows=every, first=False):
        v = v_of(pl.ds(pl.multiple_of(blk * tk, tk), tk))
        rep = acc_ref.shape[-1] // LANES
        for c in range(n_streams):
            pv = jnp.dot(p_ref[par, c, rows], v, preferred_element_type=F32)
            if first:
                acc_ref[c, rows] = pv
            else:
                acc_ref[c, rows] = acc_ref[c, rows] * jnp.tile(a_ref[par, c, rows], (1, rep)) + pv

    return score, softmax, accumulate


def _attn_scratch(n_streams, tq, tk, vw):
    return [pltpu.VMEM((N_BUF, n_streams, tq, tk), F32),
            pltpu.VMEM((N_BUF, n_streams, tq, tk), BF16),
            pltpu.VMEM((N_BUF, n_streams, tq, LANES), F32),
            pltpu.VMEM((n_streams, tq, LANES), F32),
            pltpu.VMEM((n_streams, tq, LANES), F32),
            pltpu.VMEM((n_streams, tq, vw), F32)]


def _da_attn_kernel(q_ref, k_ref, v_ref, lam_ref, sub_ref, o_ref, s_ref, p_ref, a_ref, m_ref, l_ref, acc_ref,
                    *, tk, ratio, lambda_init):
    qi = pl.program_id(2)
    hd = DA_HEAD_DIM

    stages = _attn_stages(
        2, tk,
        lambda c, rows: q_ref[rows, c * hd:(c + 1) * hd],
        lambda c, keys: k_ref[keys, c * hd:(c + 1) * hd],
        lambda rows: v_ref[rows, :],
        s_ref, p_ref, a_ref, m_ref, l_ref, acc_ref)
    _causal_sweep(qi, tk, ratio, *stages)

    lv = lam_ref[...]
    lam = (jnp.exp(jnp.sum(lv[0:1] * lv[1:2], axis=1, keepdims=True))
           - jnp.exp(jnp.sum(lv[2:3] * lv[3:4], axis=1, keepdims=True)) + lambda_init)
    l0 = jnp.sum(l_ref[0], axis=1, keepdims=True)
    l1 = jnp.sum(l_ref[1], axis=1, keepdims=True)
    o = acc_ref[0] / l0 - lam * (acc_ref[1] / l1)
    o_ref[...] = (_rms(o, sub_ref[...]) * (1.0 - lambda_init)).astype(o_ref.dtype)


DA_RATIO = 2
MLA_RATIO = 4


def _attn_blocks(seq, ratio, tk_target=512):
    tk = _pick(seq // ratio, tk_target)
    assert ratio % 2 == 0 and seq % (ratio * tk) == 0 and tk % LANES == 0
    return ratio * tk, tk


def _da_attention(qkv, lam_vecs, subln, lambda_init, batch, seq):
    d3 = qkv.shape[1]
    d = d3 // 3
    hw = 2 * DA_HEAD_DIM
    heads = d // hw
    tq, tk = _attn_blocks(seq, DA_RATIO)
    qkv3 = qkv.reshape(batch, seq, d3)
    kern = functools.partial(_da_attn_kernel, tk=tk, ratio=DA_RATIO, lambda_init=lambda_init)
    resident = pl.Buffered(1)
    out = pl.pallas_call(
        kern,
        grid=(batch, heads, seq // tq),
        in_specs=[pl.BlockSpec((None, tq, hw), lambda b, h, i: (b, i, h)),
                  pl.BlockSpec((None, seq, hw), lambda b, h, i: (b, 0, heads + h), pipeline_mode=resident),
                  pl.BlockSpec((None, seq, hw), lambda b, h, i: (b, 0, 2 * heads + h), pipeline_mode=resident),
                  pl.BlockSpec((4, DA_HEAD_DIM), lambda b, h, i: (0, 0)),
                  pl.BlockSpec((1, hw), lambda b, h, i: (0, 0))],
        out_specs=pl.BlockSpec((None, tq, hw), lambda b, h, i: (b, i, h)),
        out_shape=jax.ShapeDtypeStruct((batch, seq, d), BF16),
        scratch_shapes=_attn_scratch(2, tq, tk, hw),
        compiler_params=_cparams("parallel", "parallel", "arbitrary", vmem=ATTN_VMEM_LIMIT),
        name="da_attn",
    )(qkv3, qkv3, qkv3, lam_vecs, subln.reshape(1, hw))
    return out.reshape(batch * seq, d)


def _mla_proj_kernel(x_ref, g_ref, wd_ref, qn_ref, kvn_ref, wuq_ref, wukv_ref, tab_ref,
                     q_ref, kv_ref, kr_ref, *, q_lora, kv_lora, heads, q_scale):
    h = _rms(x_ref[...], g_ref[...]).astype(BF16)
    d = jnp.dot(h, wd_ref[...], preferred_element_type=F32)
    rc = tab_ref[:, 0:LANES]
    rs = tab_ref[:, LANES:2 * LANES]
    c_q = _rms(d[:, :q_lora], qn_ref[...]).astype(BF16)
    c_kv = _rms(d[:, q_lora:q_lora + kv_lora], kvn_ref[...]).astype(BF16)
    kr = _rope_chunk(d[:, q_lora + kv_lora:q_lora + kv_lora + LANES], rc, rs)
    kr_ref[...] = kr.astype(kr_ref.dtype)
    w = MLA_QK_PAD
    qc, qs = rc * q_scale, rs * q_scale
    for hh in range(heads):
        qh = jnp.dot(c_q, wuq_ref[:, hh * w:(hh + 1) * w], preferred_element_type=F32)
        q_ref[:, hh * w:hh * w + LANES] = (qh[:, :LANES] * q_scale).astype(q_ref.dtype)
        q_ref[:, hh * w + LANES:(hh + 1) * w] = _rope_chunk(qh[:, LANES:], qc, qs).astype(q_ref.dtype)
        kv_ref[:, hh * w:(hh + 1) * w] = jnp.dot(
            c_kv, wukv_ref[:, hh * w:(hh + 1) * w], preferred_element_type=F32).astype(kv_ref.dtype)


def _mla_proj(x, g, wd_pad, qn, kvn, wuq_pad, wukv, tab, heads, tm_target=256):
    m, d = x.shape
    tm = _pick(m, tm_target)
    q_lora, kv_lora = qn.shape[0], kvn.shape[0]
    q_scale = LOG2E / math.sqrt(MLA_NOPE + MLA_ROPE)
    kern = functools.partial(_mla_proj_kernel, q_lora=q_lora, kv_lora=kv_lora, heads=heads, q_scale=q_scale)
    const = lambda i: (0, 0)
    row = lambda i: (i, 0)
    nq, nkv = wuq_pad.shape[1], wukv.shape[1]
    return pl.pallas_call(
        kern,
        grid=(m // tm,),
        in_specs=[pl.BlockSpec((tm, d), row),
                  pl.BlockSpec((1, d), const),
                  pl.BlockSpec(wd_pad.shape, const),
                  pl.BlockSpec((1, q_lora), const),
                  pl.BlockSpec((1, kv_lora), const),
                  pl.BlockSpec(wuq_pad.shape, const),
                  pl.BlockSpec(wukv.shape, const),
                  pl.BlockSpec((tm, 2 * LANES), row)],
        out_specs=[pl.BlockSpec((tm, nq), row),
                   pl.BlockSpec((tm, nkv), row),
                   pl.BlockSpec((tm, LANES), row)],
        out_shape=[jax.ShapeDtypeStruct((m, nq), BF16),
                   jax.ShapeDtypeStruct((m, nkv), BF16),
                   jax.ShapeDtypeStruct((m, LANES), BF16)],
        compiler_params=_cparams("parallel"),
        name="mla_proj",
    )(x, g.reshape(1, d), wd_pad, qn.reshape(1, q_lora), kvn.reshape(1, kv_lora), wuq_pad, wukv, tab)


def _mla_attn_kernel(q_ref, kn_ref, kr_ref, v_ref, o_ref, s_ref, p_ref, a_ref, m_ref, l_ref, acc_ref,
                     *, tk, ratio):
    qi = pl.program_id(2)

    stages = _attn_stages(
        1, tk,
        lambda c, rows: q_ref[rows, :],
        lambda c, keys: jnp.concatenate([kn_ref[keys, :], kr_ref[keys, :]], axis=1),
        lambda rows: v_ref[rows, :],
        s_ref, p_ref, a_ref, m_ref, l_ref, acc_ref)
    _causal_sweep(qi, tk, ratio, *stages)
    o_ref[...] = (acc_ref[0] / jnp.sum(l_ref[0], axis=1, keepdims=True)).astype(o_ref.dtype)


def _mla_attention(q, kv, kr, batch, seq, heads):
    tq, tk = _attn_blocks(seq, MLA_RATIO)
    w = MLA_QK_PAD
    q3 = q.reshape(batch, seq, heads * w)
    kv3 = kv.reshape(batch, seq, heads * w)
    kr3 = kr.reshape(batch, seq, LANES)
    kern = functools.partial(_mla_attn_kernel, tk=tk, ratio=MLA_RATIO)
    out = pl.pallas_call(
        kern,
        grid=(batch, heads, seq // tq),
        in_specs=[pl.BlockSpec((None, tq, w), lambda b, h, i: (b, i, h)),
                  pl.BlockSpec((None, seq, LANES), lambda b, h, i: (b, 0, 2 * h)),
                  pl.BlockSpec((None, seq, LANES), lambda b, h, i: (b, 0, 0)),
                  pl.BlockSpec((None, seq, LANES), lambda b, h, i: (b, 0, 2 * h + 1))],
        out_specs=pl.BlockSpec((None, tq, MLA_V), lambda b, h, i: (b, i, h)),
        out_shape=jax.ShapeDtypeStruct((batch, seq, heads * MLA_V), BF16),
        scratch_shapes=_attn_scratch(1, tq, tk, MLA_V),
        compiler_params=_cparams("parallel", "parallel", "arbitrary", vmem=ATTN_VMEM_LIMIT),
        name="mla_attn",
    )(q3, kv3, kr3, kv3)
    return out.reshape(batch * seq, heads * MLA_V)


def _post_attn_kernel(o_ref, wao_ref, x_ref, g_ref, wq_ref, kv_ref, wo_ref, out_ref, *, q_scale):
    x = x_ref[...] + jnp.dot(o_ref[...], wao_ref[...], preferred_element_type=F32)
    h = _rms(x, g_ref[...]).astype(BF16)
    q = (jnp.dot(h, wq_ref[...], preferred_element_type=F32) * q_scale).astype(BF16)
    hd = CA_HEAD_DIM
    kw = CA_HEADS * hd
    outs = []
    for hh in range(CA_HEADS):
        k = kv_ref[:, hh * hd:(hh + 1) * hd]
        v = kv_ref[:, kw + hh * hd:kw + (hh + 1) * hd]
        s = lax.dot_general(q[:, hh * hd:(hh + 1) * hd], k, (((1,), (1,)), ((), ())),
                            preferred_element_type=F32)
        p = jnp.exp2(s - jnp.max(s, axis=1, keepdims=True))
        l = jnp.sum(p, axis=1, keepdims=True)
        outs.append((jnp.dot(p.astype(BF16), v, preferred_element_type=F32) / l).astype(BF16))
    o = jnp.concatenate(outs, axis=1)
    out_ref[...] = x + jnp.dot(o, wo_ref[...], preferred_element_type=F32)


def _post_attn(o, wao, x, g, wq, kv_all, layer, wo, seq, tm_target=512):
    m, d = x.shape
    tm = _pick(seq, tm_target)
    blocks_per_batch = seq // tm
    n_mem = kv_all.shape[1]
    kvw = 2 * CA_HEADS * CA_HEAD_DIM
    kern = functools.partial(_post_attn_kernel, q_scale=LOG2E / math.sqrt(CA_HEAD_DIM))
    const = lambda i: (0, 0)
    once = pl.Buffered(1)
    return pl.pallas_call(
        kern,
        grid=(m // tm,),
        in_specs=[pl.BlockSpec((tm, o.shape[1]), lambda i: (i, 0)),
                  pl.BlockSpec(wao.shape, const, pipeline_mode=once),
                  pl.BlockSpec((tm, d), lambda i: (i, 0)),
                  pl.BlockSpec((1, d), const),
                  pl.BlockSpec(wq.shape, const, pipeline_mode=once),
                  pl.BlockSpec((None, n_mem, kvw), lambda i: (i // blocks_per_batch, 0, layer)),
                  pl.BlockSpec(wo.shape, const, pipeline_mode=once)],
        out_specs=pl.BlockSpec((tm, d), lambda i: (i, 0)),
        out_shape=jax.ShapeDtypeStruct((m, d), F32),
        compiler_params=_cparams("parallel"),
        name="post_attn",
    )(o, wao, x, g.reshape(1, d), wq, kv_all, wo)


def _mlp_kernel(x_ref, g_ref, wup_ref, wdown_ref, fg_ref, out_ref, h_ref, *, final_norm):
    f = pl.program_id(1)

    @pl.when(f == 0)
    def _():
        x = x_ref[...]
        h_ref[...] = _rms(x, g_ref[...]).astype(BF16)
        out_ref[...] = x

    u = jnp.maximum(jnp.dot(h_ref[...], wup_ref[...], preferred_element_type=F32), 0.0)
    out_ref[...] += jnp.dot((u * u).astype(BF16), wdown_ref[...], preferred_element_type=F32)

    if final_norm:
        @pl.when(f == pl.num_programs(1) - 1)
        def _():
            out_ref[...] = _rms(out_ref[...], fg_ref[...])


def _mlp(x, g, wup_all, wdown_all, layer, final_g, final_norm, tm_target=1024, tf_target=512):
    m, d = x.shape
    ff = wup_all.shape[2]
    tm, tf = _pick(m, tm_target), _pick(ff, tf_target)
    kern = functools.partial(_mlp_kernel, final_norm=final_norm)
    return pl.pallas_call(
        kern,
        grid=(m // tm, ff // tf),
        in_specs=[pl.BlockSpec((tm, d), lambda i, f: (i, 0)),
                  pl.BlockSpec((1, d), lambda i, f: (0, 0)),
                  pl.BlockSpec((None, d, tf), lambda i, f: (layer, 0, f)),
                  pl.BlockSpec((None, tf, d), lambda i, f: (layer, f, 0)),
                  pl.BlockSpec((1, d), lambda i, f: (0, 0))],
        out_specs=pl.BlockSpec((tm, d), lambda i, f: (i, 0)),
        out_shape=jax.ShapeDtypeStruct((m, d), F32),
        scratch_shapes=[pltpu.VMEM((tm, d), BF16)],
        compiler_params=_cparams("parallel", "arbitrary"),
        name="mlp",
    )(x, g.reshape(1, d), wup_all, wdown_all, final_g.reshape(1, d))


def kernel(x, mem, positions, attn_norm, cross_norm, mlp_norm, mem_norm, final_norm, da_wqkv, da_lambda, da_subln, da_wo, mla_wdown, mla_q_norm, mla_kv_norm, mla_wuq, mla_wukv, mla_wo, ca_wq, ca_wkv, ca_wo, mlp_wup, mlp_wdown):
    batch, seq, d = x.shape
    depth = attn_norm.shape[0]
    n_mem = mem.shape[1]
    mla_heads = mla_wo.shape[1] // MLA_V

    tab_da = _rope_table(positions, DA_ROT)
    tab_mla = _rope_table(positions, MLA_ROPE)

    wkv_all = jnp.concatenate([ca_wkv[i].astype(BF16) for i in range(depth)], axis=1)
    kv_all = _norm_matmul(mem.reshape(batch * n_mem, d), mem_norm, wkv_all, BF16)
    kv_all = kv_all.reshape(batch, n_mem, wkv_all.shape[1])

    half = MLA_ROPE // 2

    def spread_rope(w):
        gap = jnp.zeros(w.shape[:-1] + (HALF_LANES - half,), w.dtype)
        return jnp.concatenate([w[..., :half], gap, w[..., half:], gap], axis=-1)

    wup_all, wdown_all = mlp_wup.astype(BF16), mlp_wdown.astype(BF16)
    xs = x.reshape(batch * seq, d)
    for i in range(depth):
        j = i // 2
        if i % 2 == 0:
            lambda_init = 0.8 - 0.6 * math.exp(-0.3 * i)
            w = jnp.take(da_wqkv[j], _da_weight_columns(d), axis=1).astype(BF16)
            qkv = _da_qkv(xs, attn_norm[i], w, tab_da)
            o = _da_attention(qkv, da_lambda[j], da_subln[j], lambda_init, batch, seq)
            wao = da_wo[j]
        else:
            q_lora = mla_q_norm.shape[1]
            wd = mla_wdown[j].astype(BF16)
            n_lat = wd.shape[1] - MLA_ROPE
            wd_pad = jnp.concatenate([wd[:, :n_lat], spread_rope(wd[:, n_lat:])], axis=1)
            wuq = mla_wuq[j].astype(BF16).reshape(q_lora, mla_heads, MLA_NOPE + MLA_ROPE)
            wuq_pad = jnp.concatenate([wuq[..., :MLA_NOPE], spread_rope(wuq[..., MLA_NOPE:])], axis=-1)
            wuq_pad = wuq_pad.reshape(q_lora, mla_heads * MLA_QK_PAD)
            q, kv, kr = _mla_proj(xs, attn_norm[i], wd_pad, mla_q_norm[j], mla_kv_norm[j], wuq_pad,
                                  mla_wukv[j].astype(BF16), tab_mla, mla_heads)
            o = _mla_attention(q, kv, kr, batch, seq, mla_heads)
            wao = mla_wo[j]
        xs = _post_attn(o, wao.astype(BF16), xs, cross_norm[i], ca_wq[i].astype(BF16), kv_all, i,
                        ca_wo[i].astype(BF16), seq)
        xs = _mlp(xs, mlp_norm[i], wup_all, wdown_all, i, final_norm, final_norm=(i == depth - 1))
    return xs.reshape(batch, seq, d)
```

```python
import functools
import math

import jax
import jax.numpy as jnp
import numpy as np
from jax import lax
from jax.experimental import pallas as pl
from jax.experimental.pallas import tpu as pltpu

F32 = jnp.float32
BF16 = jnp.bfloat16

EPS = 1e-6
ROPE_THETA = 500000.0
LANES = 128
LOG2E = 1.4426950408889634
NEG_BIG = -1e30

DA_HEAD_DIM = 128
DA_ROT = DA_HEAD_DIM // 4
MLA_NOPE = 128
MLA_ROPE = 64
MLA_V = 128
MLA_QK_PAD = 256
CA_HEADS = 4
CA_HEAD_DIM = 128

VMEM_LIMIT = 56 * 1024 * 1024
ATTN_VMEM_LIMIT = 60 * 1024 * 1024


def _cparams(*sem, vmem=VMEM_LIMIT):
    return pltpu.CompilerParams(dimension_semantics=sem, vmem_limit_bytes=vmem)


def _pick(n, target):
    t = min(n, target)
    while n % t or (t % 8 and t != n):
        t -= 1
    return t


def _rms(xf, g):
    ms = jnp.mean(xf * xf, axis=-1, keepdims=True)
    return xf * lax.rsqrt(ms + EPS) * g


HALF_LANES = LANES // 2


def _rope_chunk(xc, c, s):
    return xc * c + pltpu.roll(xc, HALF_LANES, 1) * s


def _rope_lane_perm(rot):
    half = rot // 2
    perm = list(range(LANES))
    perm[half:rot], perm[HALF_LANES:HALF_LANES + half] = perm[HALF_LANES:HALF_LANES + half], perm[half:rot]
    return perm


def _da_weight_columns(d):
    perm = _rope_lane_perm(DA_ROT)
    cols = []
    for chunk in range(3 * d // LANES):
        lanes = perm if chunk < 2 * d // LANES else range(LANES)
        cols.extend(chunk * LANES + lane for lane in lanes)
    return np.asarray(cols, np.int32)


def _rope_table(positions, rot):
    half = rot // 2
    inv_freq = ROPE_THETA ** (-jnp.arange(0, rot, 2, dtype=F32) / rot)
    ang = positions.astype(F32).reshape(-1)[:, None] * inv_freq
    cos, sin = jnp.cos(ang), jnp.sin(ang)
    n = ang.shape[0]
    src = jnp.concatenate([cos, sin, -sin, jnp.ones((n, 1), F32), jnp.zeros((n, 1), F32)], axis=1)
    one, zero = 3 * half, 3 * half + 1
    c_idx = [j if j < half else one for j in range(HALF_LANES)] * 2
    s_idx = ([2 * half + j if j < half else zero for j in range(HALF_LANES)]
             + [half + j if j < half else zero for j in range(HALF_LANES)])
    return jnp.take(src, np.asarray(c_idx + s_idx, np.int32), axis=1)


def _norm_matmul_kernel(x_ref, g_ref, w_ref, o_ref, h_ref):
    @pl.when(pl.program_id(1) == 0)
    def _():
        h_ref[...] = _rms(x_ref[...], g_ref[...]).astype(BF16)

    o_ref[...] = jnp.dot(h_ref[...], w_ref[...], preferred_element_type=F32).astype(o_ref.dtype)


def _norm_matmul(x, g, w, out_dtype, tm_target=512, tn_target=1024):
    m, d = x.shape
    n = w.shape[1]
    tm, tn = _pick(m, tm_target), _pick(n, tn_target)
    return pl.pallas_call(
        _norm_matmul_kernel,
        grid=(m // tm, n // tn),
        in_specs=[pl.BlockSpec((tm, d), lambda i, j: (i, 0)),
                  pl.BlockSpec((1, d), lambda i, j: (0, 0)),
                  pl.BlockSpec((d, tn), lambda i, j: (0, j))],
        out_specs=pl.BlockSpec((tm, tn), lambda i, j: (i, j)),
        out_shape=jax.ShapeDtypeStruct((m, n), out_dtype),
        scratch_shapes=[pltpu.VMEM((tm, d), BF16)],
        compiler_params=_cparams("parallel", "arbitrary"),
        name="norm_matmul",
    )(x, g.reshape(1, d), w)


def _da_qkv_kernel(x_ref, g_ref, w_ref, tab_ref, o_ref, h_ref, *, n_head_blocks, q_scale):
    j = pl.program_id(1)

    @pl.when(j == 0)
    def _():
        h_ref[...] = _rms(x_ref[...], g_ref[...]).astype(BF16)

    acc = jnp.dot(h_ref[...], w_ref[...], preferred_element_type=F32)
    rotary = j < 2 * n_head_blocks
    scale = jnp.where(j < n_head_blocks, q_scale, 1.0).astype(F32)
    c = jnp.where(rotary, tab_ref[:, 0:LANES], 1.0) * scale
    s = jnp.where(rotary, tab_ref[:, LANES:2 * LANES], 0.0) * scale
    for ch in range(acc.shape[1] // LANES):
        cols = slice(ch * LANES, (ch + 1) * LANES)
        o_ref[:, cols] = _rope_chunk(acc[:, cols], c, s).astype(o_ref.dtype)


def _da_qkv(x, g, w, tab, tm_target=1024, tn_target=1024):
    m, d = x.shape
    n = w.shape[1]
    tm, tn = _pick(m, tm_target), _pick(d, tn_target)
    q_scale = LOG2E / math.sqrt(DA_HEAD_DIM)
    kern = functools.partial(_da_qkv_kernel, n_head_blocks=d // tn, q_scale=q_scale)
    return pl.pallas_call(
        kern,
        grid=(m // tm, n // tn),
        in_specs=[pl.BlockSpec((tm, d), lambda i, j: (i, 0)),
                  pl.BlockSpec((1, d), lambda i, j: (0, 0)),
                  pl.BlockSpec((d, tn), lambda i, j: (0, j)),
                  pl.BlockSpec((tm, 2 * LANES), lambda i, j: (i, 0))],
        out_specs=pl.BlockSpec((tm, tn), lambda i, j: (i, j)),
        out_shape=jax.ShapeDtypeStruct((m, n), BF16),
        scratch_shapes=[pltpu.VMEM((tm, d), BF16)],
        compiler_params=_cparams("parallel", "arbitrary"),
        name="da_qkv",
    )(x, g.reshape(1, d), w, tab)


N_BUF = 2


def _causal_sweep(qi, tk, ratio, score, softmax, accumulate):
    first = ratio * qi
    below = [slice(j * tk, ratio * tk) for j in range(ratio)]

    def diagonal():
        score(first, 0)
        softmax(0, 0, first=True)
        for j in range(1, ratio):
            score(first + j, j % N_BUF, rows=below[j])
            accumulate(first + j - 1, (j - 1) % N_BUF, rows=below[j - 1], first=j == 1)
            softmax(j % N_BUF, 0, rows=below[j])
        accumulate(first + ratio - 1, (ratio - 1) % N_BUF, rows=below[ratio - 1])

    pl.when(qi == 0)(diagonal)

    @pl.when(qi > 0)
    def _():
        diagonal()
        score(0, 0)
        softmax(0, None)
        score(1, 1)

        def pair(u, carry):
            t = 2 * u
            accumulate(t, 0)
            softmax(1, None)
            score(t + 2, 0)
            accumulate(t + 1, 1)
            softmax(0, None)
            score(t + 3, 1)
            return carry

        lax.fori_loop(0, first // 2 - 1, pair, 0)
        t = first - 2
        accumulate(t, 0)
        softmax(1, None)
        accumulate(t + 1, 1)


def _softmax_update(s, diag, m_prev, l_prev):
    tq, tk = s.shape
    if diag is not None:
        row = lax.broadcasted_iota(jnp.int32, (tq, tk), 0)
        col = lax.broadcasted_iota(jnp.int32, (tq, tk), 1)
        s = jnp.where(col + diag <= row, s, NEG_BIG)
    m_blk = jnp.max(s, axis=1, keepdims=True)
    first = m_prev is None
    m_next = jnp.broadcast_to(m_blk, (tq, LANES)) if first else jnp.maximum(m_prev, m_blk)
    p = jnp.exp2(s - jnp.tile(m_next, (1, tk // LANES)))
    part = p[:, 0:LANES]
    for ch in range(1, tk // LANES):
        part = part + p[:, ch * LANES:(ch + 1) * LANES]
    if first:
        return m_next, part, None, p.astype(BF16)
    alpha = jnp.exp2(m_prev - m_next)
    return m_next, alpha * l_prev + part, alpha, p.astype(BF16)


def _attn_stages(n_streams, tk, q_of, k_of, v_of, s_ref, p_ref, a_ref, m_ref, l_ref, acc_ref):
    nt = (((1,), (1,)), ((), ()))
    every = slice(None)

    def score(blk, par, rows=every):
        keys = pl.ds(pl.multiple_of(blk * tk, tk), tk)
        for c in range(n_streams):
            s_ref[par, c, rows] = lax.dot_general(q_of(c, rows), k_of(c, keys), nt, preferred_element_type=F32)

    def softmax(par, diag, rows=every, first=False):
        for c in range(n_streams):
            m_prev, l_prev = (None, None) if first else (m_ref[c, rows], l_ref[c, rows])
            m_next, l_next, alpha, p = _softmax_update(s_ref[par, c, rows], diag, m_prev, l_prev)
            m_ref[c, rows] = m_next
            l_ref[c, rows] = l_next
            if not first:
                a_ref[par, c, rows] = alpha
            p_ref[par, c, rows] = p

    def accumulate(blk, par, rows=every, first=False):
        v = v_of(pl.ds(pl.multiple_of(blk * tk, tk), tk))
        rep = acc_ref.shape[-1] // LANES
        for c in range(n_streams):
            pv = jnp.dot(p_ref[par, c, rows], v, preferred_element_type=F32)
            if first:
                acc_ref[c, rows] = pv
            else:
                acc_ref[c, rows] = acc_ref[c, rows] * jnp.tile(a_ref[par, c, rows], (1, rep)) + pv

    return score, softmax, accumulate


def _attn_scratch(n_streams, tq, tk, vw):
    return [pltpu.VMEM((N_BUF, n_streams, tq, tk), F32),
            pltpu.VMEM((N_BUF, n_streams, tq, tk), BF16),
            pltpu.VMEM((N_BUF, n_streams, tq, LANES), F32),
            pltpu.VMEM((n_streams, tq, LANES), F32),
            pltpu.VMEM((n_streams, tq, LANES), F32),
            pltpu.VMEM((n_streams, tq, vw), F32)]


def _da_attn_kernel(q_ref, k_ref, v_ref, lam_ref, sub_ref, o_ref, s_ref, p_ref, a_ref, m_ref, l_ref, acc_ref,
                    *, tk, ratio, lambda_init):
    qi = pl.program_id(2)
    hd = DA_HEAD_DIM

    stages = _attn_stages(
        2, tk,
        lambda c, rows: q_ref[rows, c * hd:(c + 1) * hd],
        lambda c, keys: k_ref[keys, c * hd:(c + 1) * hd],
        lambda rows: v_ref[rows, :],
        s_ref, p_ref, a_ref, m_ref, l_ref, acc_ref)
    _causal_sweep(qi, tk, ratio, *stages)

    lv = lam_ref[...]
    lam = (jnp.exp(jnp.sum(lv[0:1] * lv[1:2], axis=1, keepdims=True))
           - jnp.exp(jnp.sum(lv[2:3] * lv[3:4], axis=1, keepdims=True)) + lambda_init)
    l0 = jnp.sum(l_ref[0], axis=1, keepdims=True)
    l1 = jnp.sum(l_ref[1], axis=1, keepdims=True)
    o = acc_ref[0] / l0 - lam * (acc_ref[1] / l1)
    o_ref[...] = (_rms(o, sub_ref[...]) * (1.0 - lambda_init)).astype(o_ref.dtype)


DA_RATIO = 2
MLA_RATIO = 4


def _attn_blocks(seq, ratio, tk_target=512):
    tk = _pick(seq // ratio, tk_target)
    assert ratio % 2 == 0 and seq % (ratio * tk) == 0 and tk % LANES == 0
    return ratio * tk, tk


def _da_attention(qkv, lam_vecs, subln, lambda_init, batch, seq):
    d3 = qkv.shape[1]
    d = d3 // 3
    hw = 2 * DA_HEAD_DIM
    heads = d // hw
    tq, tk = _attn_blocks(seq, DA_RATIO)
    qkv3 = qkv.reshape(batch, seq, d3)
    kern = functools.partial(_da_attn_kernel, tk=tk, ratio=DA_RATIO, lambda_init=lambda_init)
    resident = pl.Buffered(1)
    out = pl.pallas_call(
        kern,
        grid=(batch, heads, seq // tq),
        in_specs=[pl.BlockSpec((None, tq, hw), lambda b, h, i: (b, i, h)),
                  pl.BlockSpec((None, seq, hw), lambda b, h, i: (b, 0, heads + h), pipeline_mode=resident),
                  pl.BlockSpec((None, seq, hw), lambda b, h, i: (b, 0, 2 * heads + h)),
                  pl.BlockSpec((4, DA_HEAD_DIM), lambda b, h, i: (0, 0)),
                  pl.BlockSpec((1, hw), lambda b, h, i: (0, 0))],
        out_specs=pl.BlockSpec((None, tq, hw), lambda b, h, i: (b, i, h)),
        out_shape=jax.ShapeDtypeStruct((batch, seq, d), BF16),
        scratch_shapes=_attn_scratch(2, tq, tk, hw),
        compiler_params=_cparams("parallel", "parallel", "arbitrary", vmem=ATTN_VMEM_LIMIT),
        name="da_attn",
    )(qkv3, qkv3, qkv3, lam_vecs, subln.reshape(1, hw))
    return out.reshape(batch * seq, d)


def _mla_proj_kernel(x_ref, g_ref, wd_ref, qn_ref, kvn_ref, wuq_ref, wukv_ref, tab_ref,
                     q_ref, kv_ref, kr_ref, *, q_lora, kv_lora, heads, q_scale):
    h = _rms(x_ref[...], g_ref[...]).astype(BF16)
    d = jnp.dot(h, wd_ref[...], preferred_element_type=F32)
    rc = tab_ref[:, 0:LANES]
    rs = tab_ref[:, LANES:2 * LANES]
    c_q = _rms(d[:, :q_lora], qn_ref[...]).astype(BF16)
    c_kv = _rms(d[:, q_lora:q_lora + kv_lora], kvn_ref[...]).astype(BF16)
    kr = _rope_chunk(d[:, q_lora + kv_lora:q_lora + kv_lora + LANES], rc, rs)
    kr_ref[...] = kr.astype(kr_ref.dtype)
    w = MLA_QK_PAD
    qc, qs = rc * q_scale, rs * q_scale
    for hh in range(heads):
        qh = jnp.dot(c_q, wuq_ref[:, hh * w:(hh + 1) * w], preferred_element_type=F32)
        q_ref[:, hh * w:hh * w + LANES] = (qh[:, :LANES] * q_scale).astype(q_ref.dtype)
        q_ref[:, hh * w + LANES:(hh + 1) * w] = _rope_chunk(qh[:, LANES:], qc, qs).astype(q_ref.dtype)
        kv_ref[:, hh * w:(hh + 1) * w] = jnp.dot(
            c_kv, wukv_ref[:, hh * w:(hh + 1) * w], preferred_element_type=F32).astype(kv_ref.dtype)


def _mla_proj(x, g, wd_pad, qn, kvn, wuq_pad, wukv, tab, heads, tm_target=256):
    m, d = x.shape
    tm = _pick(m, tm_target)
    q_lora, kv_lora = qn.shape[0], kvn.shape[0]
    q_scale = LOG2E / math.sqrt(MLA_NOPE + MLA_ROPE)
    kern = functools.partial(_mla_proj_kernel, q_lora=q_lora, kv_lora=kv_lora, heads=heads, q_scale=q_scale)
    const = lambda i: (0, 0)
    row = lambda i: (i, 0)
    nq, nkv = wuq_pad.shape[1], wukv.shape[1]
    return pl.pallas_call(
        kern,
        grid=(m // tm,),
        in_specs=[pl.BlockSpec((tm, d), row),
                  pl.BlockSpec((1, d), const),
                  pl.BlockSpec(wd_pad.shape, const),
                  pl.BlockSpec((1, q_lora), const),
                  pl.BlockSpec((1, kv_lora), const),
                  pl.BlockSpec(wuq_pad.shape, const),
                  pl.BlockSpec(wukv.shape, const),
                  pl.BlockSpec((tm, 2 * LANES), row)],
        out_specs=[pl.BlockSpec((tm, nq), row),
                   pl.BlockSpec((tm, nkv), row),
                   pl.BlockSpec((tm, LANES), row)],
        out_shape=[jax.ShapeDtypeStruct((m, nq), BF16),
                   jax.ShapeDtypeStruct((m, nkv), BF16),
                   jax.ShapeDtypeStruct((m, LANES), BF16)],
        compiler_params=_cparams("parallel"),
        name="mla_proj",
    )(x, g.reshape(1, d), wd_pad, qn.reshape(1, q_lora), kvn.reshape(1, kv_lora), wuq_pad, wukv, tab)


def _mla_attn_kernel(q_ref, kn_ref, kr_ref, v_ref, o_ref, s_ref, p_ref, a_ref, m_ref, l_ref, acc_ref,
                     *, tk, ratio):
    qi = pl.program_id(2)

    stages = _attn_stages(
        1, tk,
        lambda c, rows: q_ref[rows, :],
        lambda c, keys: jnp.concatenate([kn_ref[keys, :], kr_ref[keys, :]], axis=1),
        lambda rows: v_ref[rows, :],
        s_ref, p_ref, a_ref, m_ref, l_ref, acc_ref)
    _causal_sweep(qi, tk, ratio, *stages)
    o_ref[...] = (acc_ref[0] / jnp.sum(l_ref[0], axis=1, keepdims=True)).astype(o_ref.dtype)


def _mla_attention(q, kv, kr, batch, seq, heads):
    tq, tk = _attn_blocks(seq, MLA_RATIO)
    w = MLA_QK_PAD
    q3 = q.reshape(batch, seq, heads * w)
    kv3 = kv.reshape(batch, seq, heads * w)
    kr3 = kr.reshape(batch, seq, LANES)
    kern = functools.partial(_mla_attn_kernel, tk=tk, ratio=MLA_RATIO)
    out = pl.pallas_call(
        kern,
        grid=(batch, heads, seq // tq),
        in_specs=[pl.BlockSpec((None, tq, w), lambda b, h, i: (b, i, h)),
                  pl.BlockSpec((None, seq, LANES), lambda b, h, i: (b, 0, 2 * h)),
                  pl.BlockSpec((None, seq, LANES), lambda b, h, i: (b, 0, 0)),
                  pl.BlockSpec((None, seq, LANES), lambda b, h, i: (b, 0, 2 * h + 1))],
        out_specs=pl.BlockSpec((None, tq, MLA_V), lambda b, h, i: (b, i, h)),
        out_shape=jax.ShapeDtypeStruct((batch, seq, heads * MLA_V), BF16),
        scratch_shapes=_attn_scratch(1, tq, tk, MLA_V),
        compiler_params=_cparams("parallel", "parallel", "arbitrary", vmem=ATTN_VMEM_LIMIT),
        name="mla_attn",
    )(q3, kv3, kr3, kv3)
    return out.reshape(batch * seq, heads * MLA_V)


def _post_attn_kernel(o_ref, wao_ref, x_ref, g_ref, wq_ref, kv_ref, wo_ref, out_ref, *, q_scale):
    x = x_ref[...] + jnp.dot(o_ref[...], wao_ref[...], preferred_element_type=F32)
    h = _rms(x, g_ref[...]).astype(BF16)
    q = (jnp.dot(h, wq_ref[...], preferred_element_type=F32) * q_scale).astype(BF16)
    hd = CA_HEAD_DIM
    kw = CA_HEADS * hd
    outs = []
    for hh in range(CA_HEADS):
        k = kv_ref[:, hh * hd:(hh + 1) * hd]
        v = kv_ref[:, kw + hh * hd:kw + (hh + 1) * hd]
        s = lax.dot_general(q[:, hh * hd:(hh + 1) * hd], k, (((1,), (1,)), ((), ())),
                            preferred_element_type=F32)
        p = jnp.exp2(s - jnp.max(s, axis=1, keepdims=True))
        l = jnp.sum(p, axis=1, keepdims=True)
        outs.append((jnp.dot(p.astype(BF16), v, preferred_element_type=F32) / l).astype(BF16))
    o = jnp.concatenate(outs, axis=1)
    out_ref[...] = x + jnp.dot(o, wo_ref[...], preferred_element_type=F32)


def _post_attn(o, wao, x, g, wq, kv_all, layer, wo, seq, tm_target=512):
    m, d = x.shape
    tm = _pick(seq, tm_target)
    blocks_per_batch = seq // tm
    n_mem = kv_all.shape[1]
    kvw = 2 * CA_HEADS * CA_HEAD_DIM
    kern = functools.partial(_post_attn_kernel, q_scale=LOG2E / math.sqrt(CA_HEAD_DIM))
    const = lambda i: (0, 0)
    once = pl.Buffered(1)
    return pl.pallas_call(
        kern,
        grid=(m // tm,),
        in_specs=[pl.BlockSpec((tm, o.shape[1]), lambda i: (i, 0)),
                  pl.BlockSpec(wao.shape, const, pipeline_mode=once),
                  pl.BlockSpec((tm, d), lambda i: (i, 0)),
                  pl.BlockSpec((1, d), const),
                  pl.BlockSpec(wq.shape, const, pipeline_mode=once),
                  pl.BlockSpec((None, n_mem, kvw), lambda i: (i // blocks_per_batch, 0, layer)),
                  pl.BlockSpec(wo.shape, const, pipeline_mode=once)],
        out_specs=pl.BlockSpec((tm, d), lambda i: (i, 0)),
        out_shape=jax.ShapeDtypeStruct((m, d), F32),
        compiler_params=_cparams("parallel"),
        name="post_attn",
    )(o, wao, x, g.reshape(1, d), wq, kv_all, wo)


def _mlp_kernel(x_ref, g_ref, wup_ref, wdown_ref, fg_ref, out_ref, h_ref, *, final_norm):
    f = pl.program_id(1)

    @pl.when(f == 0)
    def _():
        x = x_ref[...]
        h_ref[...] = _rms(x, g_ref[...]).astype(BF16)
        out_ref[...] = x

    u = jnp.maximum(jnp.dot(h_ref[...], wup_ref[...], preferred_element_type=F32), 0.0)
    out_ref[...] += jnp.dot((u * u).astype(BF16), wdown_ref[...], preferred_element_type=F32)

    if final_norm:
        @pl.when(f == pl.num_programs(1) - 1)
        def _():
            out_ref[...] = _rms(out_ref[...], fg_ref[...])


def _mlp(x, g, wup_all, wdown_all, layer, final_g, final_norm, tm_target=1024, tf_target=512):
    m, d = x.shape
    ff = wup_all.shape[2]
    tm, tf = _pick(m, tm_target), _pick(ff, tf_target)
    kern = functools.partial(_mlp_kernel, final_norm=final_norm)
    return pl.pallas_call(
        kern,
        grid=(m // tm, ff // tf),
        in_specs=[pl.BlockSpec((tm, d), lambda i, f: (i, 0)),
                  pl.BlockSpec((1, d), lambda i, f: (0, 0)),
                  pl.BlockSpec((None, d, tf), lambda i, f: (layer, 0, f)),
                  pl.BlockSpec((None, tf, d), lambda i, f: (layer, f, 0)),
                  pl.BlockSpec((1, d), lambda i, f: (0, 0))],
        out_specs=pl.BlockSpec((tm, d), lambda i, f: (i, 0)),
        out_shape=jax.ShapeDtypeStruct((m, d), F32),
        scratch_shapes=[pltpu.VMEM((tm, d), BF16)],
        compiler_params=_cparams("parallel", "arbitrary"),
        name="mlp",
    )(x, g.reshape(1, d), wup_all, wdown_all, final_g.reshape(1, d))


def kernel(x, mem, positions, attn_norm, cross_norm, mlp_norm, mem_norm, final_norm, da_wqkv, da_lambda, da_subln, da_wo, mla_wdown, mla_q_norm, mla_kv_norm, mla_wuq, mla_wukv, mla_wo, ca_wq, ca_wkv, ca_wo, mlp_wup, mlp_wdown):
    batch, seq, d = x.shape
    depth = attn_norm.shape[0]
    n_mem = mem.shape[1]
    mla_heads = mla_wo.shape[1] // MLA_V

    tab_da = _rope_table(positions, DA_ROT)
    tab_mla = _rope_table(positions, MLA_ROPE)

    wkv_all = jnp.concatenate([ca_wkv[i].astype(BF16) for i in range(depth)], axis=1)
    kv_all = _norm_matmul(mem.reshape(batch * n_mem, d), mem_norm, wkv_all, BF16)
    kv_all = kv_all.reshape(batch, n_mem, wkv_all.shape[1])

    half = MLA_ROPE // 2

    def spread_rope(w):
        gap = jnp.zeros(w.shape[:-1] + (HALF_LANES - half,), w.dtype)
        return jnp.concatenate([w[..., :half], gap, w[..., half:], gap], axis=-1)

    wup_all, wdown_all = mlp_wup.astype(BF16), mlp_wdown.astype(BF16)
    xs = x.reshape(batch * seq, d)
    for i in range(depth):
        j = i // 2
        if i % 2 == 0:
            lambda_init = 0.8 - 0.6 * math.exp(-0.3 * i)
            w = jnp.take(da_wqkv[j], _da_weight_columns(d), axis=1).astype(BF16)
            qkv = _da_qkv(xs, attn_norm[i], w, tab_da)
            o = _da_attention(qkv, da_lambda[j], da_subln[j], lambda_init, batch, seq)
            wao = da_wo[j]
        else:
            q_lora = mla_q_norm.shape[1]
            wd = mla_wdown[j].astype(BF16)
            n_lat = wd.shape[1] - MLA_ROPE
            wd_pad = jnp.concatenate([wd[:, :n_lat], spread_rope(wd[:, n_lat:])], axis=1)
            wuq = mla_wuq[j].astype(BF16).reshape(q_lora, mla_heads, MLA_NOPE + MLA_ROPE)
            wuq_pad = jnp.concatenate([wuq[..., :MLA_NOPE], spread_rope(wuq[..., MLA_NOPE:])], axis=-1)
            wuq_pad = wuq_pad.reshape(q_lora, mla_heads * MLA_QK_PAD)
            q, kv, kr = _mla_proj(xs, attn_norm[i], wd_pad, mla_q_norm[j], mla_kv_norm[j], wuq_pad,
                                  mla_wukv[j].astype(BF16), tab_mla, mla_heads)
            o = _mla_attention(q, kv, kr, batch, seq, mla_heads)
            wao = mla_wo[j]
        xs = _post_attn(o, wao.astype(BF16), xs, cross_norm[i], ca_wq[i].astype(BF16), kv_all, i,
                        ca_wo[i].astype(BF16), seq)
        xs = _mlp(xs, mlp_norm[i], wup_all, wdown_all, i, final_norm, final_norm=(i == depth - 1))
    return xs.reshape(batch, seq, d)
```
